```python
import math
import jax, jax.numpy as jnp
from jax import lax
import numpy as np

D_MODEL = 1024
BATCH = 8
SEQ = 4096
DEPTH = 2
DEC_BATCH = 32
DEC_SEQ = 1
PAST_LEN = 16384
PAGE_SIZE = 128

N_A = DEPTH // 2
N_B = DEPTH - N_A
CONV_W = 3
N_HEADS = 8
HEAD_DIM = D_MODEL // (2 * N_HEADS)
V_DIM = 2 * HEAD_DIM
QK_WIDTH = N_HEADS * 2 * HEAD_DIM
V_WIDTH = N_HEADS * V_DIM
D_FF = 4 * D_MODEL
PLE_DIM = 256
ROPE_DIM = HEAD_DIM // 4
ROPE_THETA = 500000.0
Q_BLOCK = 128
EPS = 1e-6
NEG = -1e30

kernel_name = 'yoco_shortconv_diffattn_step'


def rms_norm(x, g):
    xf = x.astype(jnp.float32)
    y = xf * lax.rsqrt(jnp.mean(xf * xf, axis=-1, keepdims=True) + EPS)
    return (y * g.astype(jnp.float32)).astype(x.dtype)


def lambda_init(layer_idx):
    return 0.8 - 0.6 * math.exp(-0.3 * layer_idx)


def partial_rope(t, pos):
    half = ROPE_DIM // 2
    inv = jnp.power(jnp.float32(ROPE_THETA), -jnp.arange(0, ROPE_DIM, 2, dtype=jnp.float32) / ROPE_DIM)
    ang = pos.astype(jnp.float32)[:, None] * inv[None, :]
    cos = jnp.cos(ang)[:, None, None, :]
    sin = jnp.sin(ang)[:, None, None, :]
    tr = t[..., :ROPE_DIM].astype(jnp.float32)
    x1, x2 = tr[..., :half], tr[..., half:]
    rot = jnp.concatenate([x1 * cos - x2 * sin, x2 * cos + x1 * sin], axis=-1).astype(t.dtype)
    return jnp.concatenate([rot, t[..., ROPE_DIM:]], axis=-1)


def short_conv_mixer(h, conv_state, w_in, conv_w, conv_b, w_out):
    seq = h.shape[1]
    gb, gc, xin = jnp.split(h @ w_in, 3, axis=-1)
    u = gc * xin
    full = jnp.concatenate([conv_state.astype(u.dtype), u], axis=1)
    conv = conv_b
    for j in range(CONV_W):
        conv = conv + full[:, j:j + seq] * conv_w[j]
    y = (gb * conv) @ w_out
    return y, full[:, -(CONV_W - 1):]


def diff_weights(s, lam):
    a = jax.nn.softmax(s, axis=-1)
    return a[:, 0] - lam * a[:, 1]


def prompt_diff_attention(q, k, v, lam):
    b, s = q.shape[:2]
    nb = s // Q_BLOCK
    scale = HEAD_DIM ** -0.5
    qb = jnp.moveaxis(q.reshape(b, nb, Q_BLOCK, N_HEADS, 2, HEAD_DIM), 1, 0)
    kpos = jnp.arange(s)

    def block(args):
        qi, bi = args
        sc = jnp.einsum('bqhcd,bkhcd->bchqk', qi, k, preferred_element_type=jnp.float32) * scale
        qpos = bi * Q_BLOCK + jnp.arange(Q_BLOCK)
        sc = jnp.where(kpos[None, :] <= qpos[:, None], sc, NEG)
        w = diff_weights(sc, lam)
        return jnp.einsum('bhqk,bkhe->bqhe', w.astype(v.dtype), v)

    out = lax.map(block, (qb, jnp.arange(nb)))
    return jnp.moveaxis(out, 0, 1).reshape(b, s, N_HEADS, V_DIM)


def sample_diff_attention(q, k_new, v_new, cache_k, cache_v, page_table, lam):
    db, t = q.shape[:2]
    scale = HEAD_DIM ** -0.5
    k_past = cache_k[page_table].reshape(db, -1, N_HEADS, 2, HEAD_DIM)
    v_past = cache_v[page_table].reshape(db, -1, N_HEADS, V_DIM)
    past = k_past.shape[1]
    s_past = jnp.einsum('bqhcd,bkhcd->bchqk', q, k_past, preferred_element_type=jnp.float32) * scale
    s_new = jnp.einsum('bqhcd,bkhcd->bchqk', q, k_new, preferred_element_type=jnp.float32) * scale
    causal = jnp.arange(t)[None, :] <= jnp.arange(t)[:, None]
    s_new = jnp.where(causal, s_new, NEG)
    w = diff_weights(jnp.concatenate([s_past, s_new], axis=-1), lam)
    out = (jnp.einsum('bhqk,bkhe->bqhe', w[..., :past].astype(v_past.dtype), v_past)
           + jnp.einsum('bhqk,bkhe->bqhe', w[..., past:].astype(v_new.dtype), v_new))
    return out


def run_trunk(x, p, conv_states, pos, attend,
              a_norm, a_w_in, a_conv_w, a_conv_b, a_w_out,
              kv_norm, w_k, w_v,
              b_norm, b_wq, b_lq1, b_lk1, b_lq2, b_lk2, b_subln, b_wo,
              mlp_norm, w_up, w_down, ple_norm, w_ple_gate, w_ple_proj, final_norm):
    bsz, seq = x.shape[:2]
    h = x
    new_conv = []
    k = v = None
    for li in range(DEPTH):
        if li < N_A:
            a = li
            y, st = short_conv_mixer(rms_norm(h, a_norm[a]), conv_states[a],
                                     a_w_in[a], a_conv_w[a], a_conv_b[a], a_w_out[a])
            new_conv.append(st)
        else:
            if li == N_A:
                hk = rms_norm(h, kv_norm)
                k = partial_rope((hk @ w_k).reshape(bsz, seq, N_HEADS, 2, HEAD_DIM), pos)
                v = (hk @ w_v).reshape(bsz, seq, N_HEADS, V_DIM)
            b = li - N_A
            lam_init = lambda_init(li)
            hq = rms_norm(h, b_norm[b])
            q = partial_rope((hq @ b_wq[b]).reshape(bsz, seq, N_HEADS, 2, HEAD_DIM), pos)
            lam = (jnp.exp(jnp.sum(b_lq1[b].astype(jnp.float32) * b_lk1[b].astype(jnp.float32)))
                   - jnp.exp(jnp.sum(b_lq2[b].astype(jnp.float32) * b_lk2[b].astype(jnp.float32)))
                   + lam_init)
            o = attend(q, k, v, lam)
            o = rms_norm(o, b_subln[b]) * (1.0 - lam_init)
            y = o.reshape(bsz, seq, V_WIDTH) @ b_wo[b]
        h = h + y
        m = rms_norm(h, mlp_norm[li])
        h = h + jnp.square(jax.nn.relu(m @ w_up[li])) @ w_down[li]
        gate = jax.nn.sigmoid(rms_norm(h, ple_norm[li]) @ w_ple_gate[li])
        h = h + gate * (p[li] @ w_ple_proj[li])
    return rms_norm(h, final_norm), jnp.stack(new_conv, axis=0), k, v


def setup_inputs(seed: int = 0) -> dict:
    key = jax.random.key(seed)
    ks = iter(list(jax.random.split(key, 40)))
    f32 = jnp.float32

    def nrm(shape, scale):
        return jax.random.normal(next(ks), shape, f32) * scale

    def gain(shape):
        return 1.0 + 0.02 * jax.random.normal(next(ks), shape, f32)

    n_pages = PAST_LEN // PAGE_SIZE
    n_pool = (DEC_BATCH * n_pages * 5) // 4
    d = {}
    d['x_prompt'] = nrm((BATCH, SEQ, D_MODEL), 1.0)
    d['x_sample'] = nrm((DEC_BATCH, DEC_SEQ, D_MODEL), 1.0)
    d['p_prompt'] = nrm((DEPTH, BATCH, SEQ, PLE_DIM), 1.0)
    d['p_sample'] = nrm((DEPTH, DEC_BATCH, DEC_SEQ, PLE_DIM), 1.0)
    d['state_conv'] = nrm((N_A, DEC_BATCH, CONV_W - 1, D_MODEL), 1.0)
    d['cache_k'] = nrm((n_pool, PAGE_SIZE, N_HEADS, 2, HEAD_DIM), 1.0)
    d['cache_v'] = nrm((n_pool, PAGE_SIZE, N_HEADS, V_DIM), 1.0)
    perm = jax.random.permutation(next(ks), n_pool)
    d['page_table'] = perm[:DEC_BATCH * n_pages].reshape(DEC_BATCH, n_pages).astype(jnp.int32)
    d['a_norm'] = gain((N_A, D_MODEL))
    d['a_w_in'] = nrm((N_A, D_MODEL, 3 * D_MODEL), D_MODEL ** -0.5)
    d['a_conv_w'] = nrm((N_A, CONV_W, D_MODEL), CONV_W ** -0.5)
    d['a_conv_b'] = nrm((N_A, D_MODEL), 0.02)
    d['a_w_out'] = nrm((N_A, D_MODEL, D_MODEL), D_MODEL ** -0.5)
    d['kv_norm'] = gain((D_MODEL,))
    d['w_k'] = nrm((D_MODEL, QK_WIDTH), D_MODEL ** -0.5)
    d['w_v'] = nrm((D_MODEL, V_WIDTH), D_MODEL ** -0.5)
    d['b_norm'] = gain((N_B, D_MODEL))
    d['b_wq'] = nrm((N_B, D_MODEL, QK_WIDTH), D_MODEL ** -0.5)
    d['b_lq1'] = nrm((N_B, HEAD_DIM), 0.1)
    d['b_lk1'] = nrm((N_B, HEAD_DIM), 0.1)
    d['b_lq2'] = nrm((N_B, HEAD_DIM), 0.1)
    d['b_lk2'] = nrm((N_B, HEAD_DIM), 0.1)
    d['b_subln'] = gain((N_B, V_DIM))
    d['b_wo'] = nrm((N_B, V_WIDTH, D_MODEL), V_WIDTH ** -0.5)
    d['mlp_norm'] = gain((DEPTH, D_MODEL))
    d['w_up'] = nrm((DEPTH, D_MODEL, D_FF), D_MODEL ** -0.5)
    d['w_down'] = nrm((DEPTH, D_FF, D_MODEL), D_FF ** -0.5)
    d['ple_norm'] = gain((DEPTH, D_MODEL))
    d['w_ple_gate'] = nrm((DEPTH, D_MODEL, D_MODEL), D_MODEL ** -0.5)
    d['w_ple_proj'] = nrm((DEPTH, PLE_DIM, D_MODEL), PLE_DIM ** -0.5)
    d['final_norm'] = gain((D_MODEL,))
    return d


def reference(x_prompt, x_sample, p_prompt, p_sample, state_conv, cache_k, cache_v, page_table,
              a_norm, a_w_in, a_conv_w, a_conv_b, a_w_out,
              kv_norm, w_k, w_v,
              b_norm, b_wq, b_lq1, b_lk1, b_lq2, b_lk2, b_subln, b_wo,
              mlp_norm, w_up, w_down, ple_norm, w_ple_gate, w_ple_proj, final_norm):
    weights = (a_norm, a_w_in, a_conv_w, a_conv_b, a_w_out,
               kv_norm, w_k, w_v,
               b_norm, b_wq, b_lq1, b_lk1, b_lq2, b_lk2, b_subln, b_wo,
               mlp_norm, w_up, w_down, ple_norm, w_ple_gate, w_ple_proj, final_norm)
    conv0 = jnp.zeros((N_A, x_prompt.shape[0], CONV_W - 1, D_MODEL), x_prompt.dtype)
    pos_p = jnp.arange(x_prompt.shape[1])
    y_prompt, conv_p, k_p, v_p = run_trunk(x_prompt, p_prompt, conv0, pos_p,
                                           prompt_diff_attention, *weights)
    pos_s = PAST_LEN + jnp.arange(x_sample.shape[1])

    def attend_sample(q, k, v, lam):
        return sample_diff_attention(q, k, v, cache_k, cache_v, page_table, lam)

    y_sample, conv_s, k_s, v_s = run_trunk(x_sample, p_sample, state_conv, pos_s,
                                           attend_sample, *weights)
    return (y_prompt, y_sample, conv_p, conv_s, k_p, v_p, k_s, v_s)
```

```python
import functools
import math

import jax
import jax.numpy as jnp
from jax import lax
from jax.experimental import pallas as pl
from jax.experimental.pallas import tpu as pltpu

F32 = jnp.float32
BF16 = jnp.bfloat16

EPS = 1e-6
NEG = -1e30
ROPE_THETA = 500000.0
CONV_W = 3

V7X_VMEM_LIMIT_BYTES = 56 * 1024 * 1024
LANES = 128

TOKEN_TILE = 512
FF_CHUNK = 1024
ATTN_BLOCK = 512
DECODE_PAGES_PER_STEP = 8


def _lambda_init(layer_idx):
    return 0.8 - 0.6 * math.exp(-0.3 * layer_idx)


def _rms_hat(x):
    return x * lax.rsqrt(jnp.mean(x * x, axis=-1, keepdims=True) + EPS)


def _dot(a, b):
    return jnp.dot(a, b, preferred_element_type=F32)


def _dot_nt(a, b):
    return lax.dot_general(a, b, (((1,), (1,)), ((), ())), preferred_element_type=F32)


def _resident(shape):
    nd = len(shape)
    return pl.BlockSpec(shape, lambda *_: (0,) * nd, pipeline_mode=pl.Buffered(1))


def _params(*sem):
    return pltpu.CompilerParams(dimension_semantics=sem, vmem_limit_bytes=V7X_VMEM_LIMIT_BYTES)


def _mixer_seq_kernel(x_ref, st_ref, g_ref, win_ref, cw_ref, cb_ref, wout_ref,
                      h_ref, stout_ref, ubuf, *, tm, d):
    @pl.when(pl.program_id(1) == 0)
    def _():
        ubuf[6:8, :] = st_ref[0]

    x = x_ref[0]
    hn = (_rms_hat(x) * g_ref[...]).astype(BF16)
    gc = _dot(hn, win_ref[:, d:2 * d])
    xin = _dot(hn, win_ref[:, 2 * d:3 * d])
    u = gc * xin
    ubuf[8:8 + tm, :] = u
    conv = (cb_ref[...] + ubuf[6:6 + tm, :] * cw_ref[0:1, :]
            + ubuf[7:7 + tm, :] * cw_ref[1:2, :] + u * cw_ref[2:3, :])
    gb = _dot(hn, win_ref[:, 0:d])
    y = _dot((gb * conv).astype(BF16), wout_ref[...])
    h_ref[0] = x + y
    last = ubuf[6 + tm:8 + tm, :]
    stout_ref[0] = last
    ubuf[6:8, :] = last


def _mixer_seq(x, state, g, w_in, conv_w, conv_b, w_out):
    b, s, d = x.shape
    tm = min(TOKEN_TILE, s)
    assert s % tm == 0 and tm >= CONV_W - 1
    kern = functools.partial(_mixer_seq_kernel, tm=tm, d=d)
    return pl.pallas_call(
        kern,
        out_shape=(jax.ShapeDtypeStruct((b, s, d), F32),
                   jax.ShapeDtypeStruct((b, CONV_W - 1, d), F32)),
        grid=(b, s // tm),
        in_specs=[
            pl.BlockSpec((1, tm, d), lambda i, t: (i, t, 0)),
            pl.BlockSpec((1, CONV_W - 1, d), lambda i, t: (i, 0, 0)),
            _resident((1, d)),
            _resident((d, 3 * d)),
            _resident((CONV_W, d)),
            _resident((1, d)),
            _resident((d, d)),
        ],
        out_specs=(pl.BlockSpec((1, tm, d), lambda i, t: (i, t, 0)),
                   pl.BlockSpec((1, CONV_W - 1, d), lambda i, t: (i, 0, 0))),
        scratch_shapes=[pltpu.VMEM((tm + 8, d), F32)],
        compiler_params=_params("arbitrary", "arbitrary"),
        name="mixer_seq",
    )(x, state, g, w_in, conv_w, conv_b, w_out)


def _mixer_step_kernel(x_ref, s0_ref, s1_ref, g_ref, win_ref, cw_ref, cb_ref, wout_ref,
                       h_ref, u_ref, *, d):
    x = x_ref[...]
    hn = (_rms_hat(x) * g_ref[...]).astype(BF16)
    gc = _dot(hn, win_ref[:, d:2 * d])
    xin = _dot(hn, win_ref[:, 2 * d:3 * d])
    u = gc * xin
    conv = (cb_ref[...] + s0_ref[...] * cw_ref[0:1, :]
            + s1_ref[...] * cw_ref[1:2, :] + u * cw_ref[2:3, :])
    gb = _dot(hn, win_ref[:, 0:d])
    y = _dot((gb * conv).astype(BF16), wout_ref[...])
    h_ref[...] = x + y
    u_ref[...] = u


def _mixer_step(x, s0, s1, g, w_in, conv_w, conv_b, w_out):
    n, d = x.shape
    kern = functools.partial(_mixer_step_kernel, d=d)
    return pl.pallas_call(
        kern,
        out_shape=(jax.ShapeDtypeStruct((n, d), F32), jax.ShapeDtypeStruct((n, d), F32)),
        compiler_params=pltpu.CompilerParams(vmem_limit_bytes=V7X_VMEM_LIMIT_BYTES),
        name="mixer_step",
    )(x, s0, s1, g, w_in, conv_w, conv_b, w_out)


def _mlp_ple_kernel(*refs, has_attn, has_final, ff, ff_chunk):
    refs = list(refs)
    h_ref = refs.pop(0)
    if has_attn:
        o_ref = refs.pop(0)
        wo_ref = refs.pop(0)
    mg_ref, wup_ref, wdown_ref, pg_ref, wgate_ref, p_ref, wproj_ref = refs[:7]
    refs = refs[7:]
    if has_final:
        fg_ref = refs.pop(0)
    out_ref = refs.pop(0)

    h = h_ref[...]
    if has_attn:
        h = h + _dot(o_ref[...], wo_ref[...])
    m = (_rms_hat(h) * mg_ref[...]).astype(BF16)
    acc = h
    for c in range(ff // ff_chunk):
        lo = c * ff_chunk
        a = _dot(m, wup_ref[:, lo:lo + ff_chunk])
        a = jnp.square(jnp.maximum(a, 0.0)).astype(BF16)
        acc = acc + _dot(a, wdown_ref[lo:lo + ff_chunk, :])
    gate = jax.nn.sigmoid(_dot((_rms_hat(acc) * pg_ref[...]).astype(BF16), wgate_ref[...]))
    h3 = acc + gate * _dot(p_ref[...].astype(BF16), wproj_ref[...])
    if has_final:
        h3 = _rms_hat(h3) * fg_ref[...]
    out_ref[...] = h3


def _mlp_ple(h, p, mg, w_up, w_down, pg, w_gate, w_proj, attn=None, final_g=None):
    n, d = h.shape
    ff = w_up.shape[1]
    ple = p.shape[1]
    tm = min(TOKEN_TILE, n)
    assert n % tm == 0
    ff_chunk = min(FF_CHUNK, ff)
    assert ff % ff_chunk == 0
    row = lambda i: (i, 0)
    args = [h]
    specs = [pl.BlockSpec((tm, d), row)]
    if attn is not None:
        o, wo = attn
        args += [o, wo]
        specs += [pl.BlockSpec((tm, o.shape[1]), row), _resident(wo.shape)]
    args += [mg, w_up, w_down, pg, w_gate, p, w_proj]
    specs += [_resident((1, d)), _resident((d, ff)), _resident((ff, d)), _resident((1, d)),
              _resident((d, d)), pl.BlockSpec((tm, ple), row), _resident((ple, d))]
    if final_g is not None:
        args.append(final_g)
        specs.append(_resident((1, d)))
    kern = functools.partial(_mlp_ple_kernel, has_attn=attn is not None,
                             has_final=final_g is not None, ff=ff, ff_chunk=ff_chunk)
    return pl.pallas_call(
        kern,
        out_shape=jax.ShapeDtypeStruct((n, d), F32),
        grid=(n // tm,),
        in_specs=specs,
        out_specs=pl.BlockSpec((tm, d), row),
        compiler_params=_params("arbitrary"),
        name="mlp_ple",
    )(*args)


def _rope_tables(pos, head_dim):
    rope_dim = head_dim // 4
    half = rope_dim // 2
    inv = jnp.power(jnp.float32(ROPE_THETA),
                    -jnp.arange(0, rope_dim, 2, dtype=jnp.float32) / rope_dim)
    ang = pos.astype(jnp.float32)[:, None] * inv[None, :]
    cos = jnp.cos(ang)
    sin = jnp.sin(ang)
    dcol = jnp.arange(2 * head_dim) % head_dim
    j = dcol % half
    c = jnp.where(dcol[None, :] < rope_dim, cos[:, j], 1.0)
    sa = jnp.where(dcol[None, :] < half, -sin[:, j], 0.0)
    sb = jnp.where((dcol[None, :] >= half) & (dcol[None, :] < rope_dim), sin[:, j], 0.0)
    return c.astype(F32), sa.astype(F32), sb.astype(F32)


def _rope_head(t, c, sa, sb, half):
    width = t.shape[-1]
    return t * c + pltpu.roll(t, width - half, 1) * sa + pltpu.roll(t, half, 1) * sb


def _qkv_kernel(h_ref, kvg_ref, qg_ref, wk_ref, wv_ref, wq_ref, c_ref, sa_ref, sb_ref,
                k_ref, v_ref, qb_ref, *attn_refs, n_heads, half, q_scale):
    h = h_ref[0]
    xh = _rms_hat(h)
    hk = (xh * kvg_ref[...]).astype(BF16)
    hq = (xh * qg_ref[...]).astype(BF16)
    c = c_ref[...]
    sa = sa_ref[...]
    sb = sb_ref[...]
    k = _dot(hk, wk_ref[...])
    q = _dot(hq, wq_ref[...])
    v = _dot(hk, wv_ref[...])
    v_ref[0] = v
    for hd in range(n_heads):
        cols = slice(hd * LANES, (hd + 1) * LANES)
        kh = _rope_head(k[:, cols], c, sa, sb, half)
        k_ref[0, :, cols] = kh
        qb_ref[0, :, cols] = (_rope_head(q[:, cols], c, sa, sb, half) * q_scale).astype(BF16)
        if attn_refs:
            attn_refs[0][0, :, cols] = kh.astype(BF16)
    if attn_refs:
        attn_refs[1][0, 0] = v.T.astype(BF16)


def _qkv(h, kvg, qg, w_k, w_v, w_q, tables, n_heads, head_dim, for_attention):
    b, s, d = h.shape
    tm = min(TOKEN_TILE, s)
    assert s % tm == 0 and 2 * head_dim == LANES
    nt = s // tm
    tok = lambda i, t: (i, t, 0)
    tab = pl.BlockSpec((tm, LANES), lambda i, t: (t, 0))
    out_shape = [jax.ShapeDtypeStruct((b, s, d), F32), jax.ShapeDtypeStruct((b, s, d), F32),
                 jax.ShapeDtypeStruct((b, s, d), BF16)]
    out_specs = [pl.BlockSpec((1, tm, d), tok)] * 3
    if for_attention:
        out_shape += [jax.ShapeDtypeStruct((b, s, d), BF16),
                      jax.ShapeDtypeStruct((b, nt, d, tm), BF16)]
        out_specs += [pl.BlockSpec((1, tm, d), tok),
                      pl.BlockSpec((1, 1, d, tm), lambda i, t: (i, t, 0, 0))]
    kern = functools.partial(_qkv_kernel, n_heads=n_heads, half=head_dim // 8,
                             q_scale=head_dim ** -0.5)
    return pl.pallas_call(
        kern,
        out_shape=tuple(out_shape),
        grid=(b, nt),
        in_specs=[pl.BlockSpec((1, tm, d), tok), _resident((1, d)), _resident((1, d)),
                  _resident((d, d)), _resident((d, d)), _resident((d, d)), tab, tab, tab],
        out_specs=tuple(out_specs),
        compiler_params=_params("arbitrary", "arbitrary"),
        name="qkv_rope",
    )(h, kvg, qg, w_k, w_v, w_q, *tables)


def _lambda_value(lq1_ref, lk1_ref, lq2_ref, lk2_ref, lam_init):
    s1 = jnp.sum(lq1_ref[...] * lk1_ref[...], axis=-1, keepdims=True)
    s2 = jnp.sum(lq2_ref[...] * lk2_ref[...], axis=-1, keepdims=True)
    return jnp.exp(s1) - jnp.exp(s2) + lam_init


def _prompt_attn_kernel(lq1_ref, lk1_ref, lq2_ref, lk2_ref, sg_ref, q_ref, k_ref, vt_ref,
                        o_ref, m_ref, l_ref, acc_ref, *, blk, head_dim, lam_init):
    qi = pl.program_id(2)
    q = q_ref[0]
    lane = lax.broadcasted_iota(jnp.int32, q.shape, 1)
    zero = jnp.zeros_like(q)
    q_comp = (jnp.where(lane < head_dim, q, zero), jnp.where(lane >= head_dim, q, zero))

    m_ref[...] = jnp.full(m_ref.shape, NEG, F32)
    l_ref[...] = jnp.zeros(l_ref.shape, F32)
    acc_ref[...] = jnp.zeros(acc_ref.shape, F32)

    def step(j, masked):
        start = pl.multiple_of(j * blk, blk)
        kb = k_ref[0, pl.ds(start, blk), :]
        vt = vt_ref[0, j]
        for c in range(2):
            s = _dot_nt(kb, q_comp[c])
            if masked:
                kpos = lax.broadcasted_iota(jnp.int32, s.shape, 0)
                qpos = lax.broadcasted_iota(jnp.int32, s.shape, 1)
                s = jnp.where(kpos <= qpos, s, NEG)
            m_old = m_ref[c]
            m_new = jnp.maximum(m_old, jnp.max(s, axis=0, keepdims=True))
            alpha = jnp.exp(m_old - m_new)
            p = jnp.exp(s - m_new)
            l_ref[c] = alpha * l_ref[c] + jnp.sum(p, axis=0, keepdims=True)
            acc_ref[c] = alpha * acc_ref[c] + _dot(vt, p.astype(BF16))
            m_ref[c] = m_new

    def body(j, carry):
        step(j, False)
        return carry

    lax.fori_loop(0, qi, body, 0)
    step(qi, True)

    lam = _lambda_value(lq1_ref, lk1_ref, lq2_ref, lk2_ref, lam_init)
    out = acc_ref[0] / l_ref[0] - lam * (acc_ref[1] / l_ref[1])
    ms = jnp.mean(out * out, axis=0, keepdims=True)
    y = out * lax.rsqrt(ms + EPS) * sg_ref[...] * (1.0 - lam_init)
    o_ref[0] = y.T.astype(BF16)


def _prompt_attention(q, k, vt, lam_vecs, sub_g, n_heads, head_dim, lam_init):
    b, s, d = q.shape
    blk = vt.shape[-1]
    nq = s // blk
    vdim = d // n_heads
    assert vdim == LANES and 2 * head_dim == LANES
    vec = _resident((1, head_dim))
    kern = functools.partial(_prompt_attn_kernel, blk=blk, head_dim=head_dim, lam_init=lam_init)
    return pl.pallas_call(
        kern,
        out_shape=jax.ShapeDtypeStruct((b, s, d), BF16),
        grid=(b, n_heads, nq),
        in_specs=[vec, vec, vec, vec, _resident((vdim, 1)),
                  pl.BlockSpec((1, blk, LANES), lambda i, h, t: (i, t, h)),
                  pl.BlockSpec((1, s, LANES), lambda i, h, t: (i, 0, h)),
                  pl.BlockSpec((1, nq, vdim, blk), lambda i, h, t: (i, 0, h, 0))],
        out_specs=pl.BlockSpec((1, blk, LANES), lambda i, h, t: (i, t, h)),
        scratch_shapes=[pltpu.VMEM((2, 1, blk), F32), pltpu.VMEM((2, 1, blk), F32),
                        pltpu.VMEM((2, vdim, blk), F32)],
        compiler_params=_params("arbitrary", "arbitrary", "arbitrary"),
        name="prompt_diff_attn",
    )(*lam_vecs, sub_g, q, k, vt)


def _decode_attn_kernel(pt_ref, lq1_ref, lk1_ref, lq2_ref, lk2_ref, sg_ref,
                        q_ref, kn_ref, vn_ref, *rest, pages, n_heads, head_dim, lam_init):
    k_refs = rest[:pages]
    v_refs = rest[pages:2 * pages]
    o_ref, m_ref, l_ref, acc_ref = rest[2 * pages:]
    del pt_ref
    j = pl.program_id(1)
    rows = 2 * n_heads
    d = q_ref.shape[-1]
    page = k_refs[0].shape[-1]

    qrow = q_ref[0].astype(F32)
    row = lax.broadcasted_iota(jnp.int32, (rows, d), 0)
    col = lax.broadcasted_iota(jnp.int32, (rows, d), 1)
    own = (col // head_dim) == 2 * (row % n_heads) + row // n_heads
    qm32 = jnp.where(own, jnp.broadcast_to(qrow, (rows, d)), 0.0)
    qm = qm32.astype(BF16)

    @pl.when(j == 0)
    def _():
        m_ref[...] = jnp.sum(qm32 * kn_ref[0], axis=1, keepdims=True)
        l_ref[...] = jnp.ones(l_ref.shape, F32)
        acc_ref[...] = jnp.concatenate([vn_ref[0], vn_ref[0]], axis=0)

    s = jnp.concatenate([_dot(qm, k_refs[g][0].astype(BF16)) for g in range(pages)], axis=1)
    m_old = m_ref[...]
    m_new = jnp.maximum(m_old, jnp.max(s, axis=1, keepdims=True))
    alpha = jnp.exp(m_old - m_new)
    p = jnp.exp(s - m_new).astype(BF16)
    l_ref[...] = alpha * l_ref[...] + jnp.sum(p.astype(F32), axis=1, keepdims=True)
    m_ref[...] = m_new
    head_of_row = lax.broadcasted_iota(jnp.int32, acc_ref.shape, 0) % n_heads
    pv = jnp.zeros(acc_ref.shape, F32)
    for hd in range(n_heads):
        r = _dot(p[:, 0:page], v_refs[0][0, :, hd, :].astype(BF16))
        for g in range(1, pages):
            r = r + _dot(p[:, g * page:(g + 1) * page], v_refs[g][0, :, hd, :].astype(BF16))
        pv = jnp.where(head_of_row == hd, r, pv)
    acc_ref[...] = alpha * acc_ref[...] + pv

    @pl.when(j == pl.num_programs(1) - 1)
    def _():
        lam = _lambda_value(lq1_ref, lk1_ref, lq2_ref, lk2_ref, lam_init)
        w = acc_ref[...] / l_ref[...]
        out = w[0:n_heads] - lam * w[n_heads:rows]
        o_ref[0] = (_rms_hat(out) * sg_ref[...] * (1.0 - lam_init)).astype(BF16)


def _decode_attention(q, k_new, v_new, cache_k, cache_v, page_table, lam_vecs, sub_g,
                      n_heads, head_dim, lam_init):
    n, d = q.shape
    n_pool, page = cache_k.shape[:2]
    vdim = cache_v.shape[-1]
    n_pages = page_table.shape[1]
    pages = math.gcd(DECODE_PAGES_PER_STEP, n_pages)
    rows = 2 * n_heads
    kt = jnp.transpose(cache_k, (0, 2, 3, 4, 1)).reshape(n_pool, d, page)

    def k_spec(g):
        return pl.BlockSpec((1, d, page), lambda i, j, pt: (pt[i, j * pages + g], 0, 0))

    def v_spec(g):
        return pl.BlockSpec((1, page, n_heads, vdim),
                            lambda i, j, pt: (pt[i, j * pages + g], 0, 0, 0))

    const = lambda shape: pl.BlockSpec(shape, lambda i, j, pt: (0,) * len(shape))
    per_seq = pl.BlockSpec((1, 1, d), lambda i, j, pt: (i, 0, 0))
    per_seq_heads = pl.BlockSpec((1, n_heads, vdim), lambda i, j, pt: (i, 0, 0))
    kern = functools.partial(_decode_attn_kernel, pages=pages, n_heads=n_heads,
                             head_dim=head_dim, lam_init=lam_init)
    grid_spec = pltpu.PrefetchScalarGridSpec(
        num_scalar_prefetch=1,
        grid=(n, n_pages // pages),
        in_specs=[const((1, head_dim))] * 4 + [const((1, vdim))]
                 + [per_seq, per_seq, per_seq_heads]
                 + [k_spec(g) for g in range(pages)] + [v_spec(g) for g in range(pages)],
        out_specs=per_seq_heads,
        scratch_shapes=[pltpu.VMEM((rows, 1), F32), pltpu.VMEM((rows, 1), F32),
                        pltpu.VMEM((rows, vdim), F32)],
    )
    out = pl.pallas_call(
        kern,
        out_shape=jax.ShapeDtypeStruct((n, n_heads, vdim), BF16),
        grid_spec=grid_spec,
        compiler_params=_params("arbitrary", "arbitrary"),
        name="decode_diff_attn",
    )(page_table, *lam_vecs, sub_g, q.reshape(n, 1, d), k_new.reshape(n, 1, d),
      v_new.reshape(n, n_heads, vdim), *([kt] * pages), *([cache_v] * pages))
    return out.reshape(n, d)


def kernel(x_prompt, x_sample, p_prompt, p_sample, state_conv, cache_k, cache_v, page_table,
           a_norm, a_w_in, a_conv_w, a_conv_b, a_w_out, kv_norm, w_k, w_v, b_norm, b_wq,
           b_lq1, b_lk1, b_lq2, b_lk2, b_subln, b_wo, mlp_norm, w_up, w_down, ple_norm,
           w_ple_gate, w_ple_proj, final_norm):
    batch, seq, d = x_prompt.shape
    dec_batch, dec_seq, _ = x_sample.shape
    depth = p_prompt.shape[0]
    n_pool, page, n_heads, two, head_dim = cache_k.shape
    n_a = state_conv.shape[0]
    assert depth == 2 and n_a == 1 and two == 2 and dec_seq == 1
    past_len = page_table.shape[1] * page
    li_b = n_a
    lam_init = _lambda_init(li_b)

    bf = lambda w: w.astype(BF16)
    row = lambda g: g.reshape(1, -1).astype(F32)
    w_in, w_out = bf(a_w_in[0]), bf(a_w_out[0])
    wk, wv, wq, wo = bf(w_k), bf(w_v), bf(b_wq[0]), bf(b_wo[0])
    wup, wdown = bf(w_up), bf(w_down)
    wgate, wproj = bf(w_ple_gate), bf(w_ple_proj)
    lam_vecs = tuple(row(v[0]) for v in (b_lq1, b_lk1, b_lq2, b_lk2))
    mlp_args = lambda li: (row(mlp_norm[li]), wup[li], wdown[li], row(ple_norm[li]),
                           wgate[li], wproj[li])

    zeros_state = jnp.zeros((batch, CONV_W - 1, d), F32)
    h, conv_p = _mixer_seq(x_prompt, zeros_state, row(a_norm[0]), w_in, a_conv_w[0],
                           row(a_conv_b[0]), w_out)
    h = _mlp_ple(h.reshape(batch * seq, d), p_prompt[0].reshape(batch * seq, -1), *mlp_args(0))
    tables_p = _rope_tables(jnp.arange(seq), head_dim)
    k_p, v_p, q_b, k_b, vt_b = _qkv(h.reshape(batch, seq, d), row(kv_norm), row(b_norm[0]),
                                    wk, wv, wq, tables_p, n_heads, head_dim, True)
    o = _prompt_attention(q_b, k_b, vt_b, lam_vecs, b_subln[0].reshape(-1, 1).astype(F32),
                          n_heads, head_dim, lam_init)
    y_prompt = _mlp_ple(h, p_prompt[1].reshape(batch * seq, -1), *mlp_args(1),
                        attn=(o.reshape(batch * seq, d), wo), final_g=row(final_norm))

    xs = x_sample.reshape(dec_batch, d)
    hs, u_s = _mixer_step(xs, state_conv[0, :, 0], state_conv[0, :, 1], row(a_norm[0]), w_in,
                          a_conv_w[0], row(a_conv_b[0]), w_out)
    conv_s = jnp.stack([state_conv[0, :, 1], u_s], axis=1)
    hs = _mlp_ple(hs, p_sample[0].reshape(dec_batch, -1), *mlp_args(0))
    tables_s = _rope_tables(jnp.full((dec_batch,), past_len, jnp.int32), head_dim)
    k_s, v_s, q_s = _qkv(hs.reshape(1, dec_batch, d), row(kv_norm), row(b_norm[0]),
                         wk, wv, wq, tables_s, n_heads, head_dim, False)
    k_s, v_s, q_s = (a.reshape(dec_batch, d) for a in (k_s, v_s, q_s))
    o_s = _decode_attention(q_s, k_s, v_s, cache_k, cache_v, page_table, lam_vecs,
                            row(b_subln[0]), n_heads, head_dim, lam_init)
    y_sample = _mlp_ple(hs, p_sample[1].reshape(dec_batch, -1), *mlp_args(1),
                        attn=(o_s, wo), final_g=row(final_norm))

    return (y_prompt.reshape(batch, seq, d),
            y_sample.reshape(dec_batch, dec_seq, d),
            conv_p[None],
            conv_s[None],
            k_p.reshape(batch, seq, n_heads, 2, head_dim),
            v_p.reshape(batch, seq, n_heads, d // n_heads),
            k_s.reshape(dec_batch, dec_seq, n_heads, 2, head_dim),
            v_s.reshape(dec_batch, dec_seq, n_heads, d // n_heads))
```

```python
import functools
import math

import jax
import jax.numpy as jnp
from jax import lax
from jax.experimental import pallas as pl
from jax.experimental.pallas import tpu as pltpu

F32 = jnp.float32
BF16 = jnp.bfloat16

EPS = 1e-6
NEG = -1e30
ROPE_THETA = 500000.0
CONV_W = 3

V7X_VMEM_LIMIT_BYTES = 56 * 1024 * 1024
LANES = 128

TOKEN_TILE = 512
FF_CHUNK = 1024
VT_ONES_ROWS = 16
LOG2E = math.log2(math.e)
DECODE_PAGES_PER_STEP = 8


def _lambda_init(layer_idx):
    return 0.8 - 0.6 * math.exp(-0.3 * layer_idx)


def _rms_hat(x):
    return x * lax.rsqrt(jnp.mean(x * x, axis=-1, keepdims=True) + EPS)


def _dot(a, b):
    return jnp.dot(a, b, preferred_element_type=F32)


def _dot_nt(a, b):
    return lax.dot_general(a, b, (((1,), (1,)), ((), ())), preferred_element_type=F32)


def _resident(shape):
    nd = len(shape)
    return pl.BlockSpec(shape, lambda *_: (0,) * nd, pipeline_mode=pl.Buffered(1))


def _params(*sem):
    return pltpu.CompilerParams(dimension_semantics=sem, vmem_limit_bytes=V7X_VMEM_LIMIT_BYTES)


def _mixer_seq_kernel(x_ref, st_ref, g_ref, win_ref, cw_ref, cb_ref, wout_ref,
                      h_ref, stout_ref, ubuf, *, tm, d):
    @pl.when(pl.program_id(1) == 0)
    def _():
        ubuf[6:8, :] = st_ref[0]

    x = x_ref[0]
    hn = (_rms_hat(x) * g_ref[...]).astype(BF16)
    gc = _dot(hn, win_ref[:, d:2 * d])
    xin = _dot(hn, win_ref[:, 2 * d:3 * d])
    u = gc * xin
    ubuf[8:8 + tm, :] = u
    conv = (cb_ref[...] + ubuf[6:6 + tm, :] * cw_ref[0:1, :]
            + ubuf[7:7 + tm, :] * cw_ref[1:2, :] + u * cw_ref[2:3, :])
    gb = _dot(hn, win_ref[:, 0:d])
    y = _dot((gb * conv).astype(BF16), wout_ref[...])
    h_ref[0] = x + y
    last = ubuf[6 + tm:8 + tm, :]
    stout_ref[0] = last
    ubuf[6:8, :] = last


def _mixer_seq(x, state, g, w_in, conv_w, conv_b, w_out):
    b, s, d = x.shape
    tm = min(TOKEN_TILE, s)
    assert s % tm == 0 and tm >= CONV_W - 1
    kern = functools.partial(_mixer_seq_kernel, tm=tm, d=d)
    return pl.pallas_call(
        kern,
        out_shape=(jax.ShapeDtypeStruct((b, s, d), F32),
                   jax.ShapeDtypeStruct((b, CONV_W - 1, d), F32)),
        grid=(b, s // tm),
        in_specs=[
            pl.BlockSpec((1, tm, d), lambda i, t: (i, t, 0)),
            pl.BlockSpec((1, CONV_W - 1, d), lambda i, t: (i, 0, 0)),
            _resident((1, d)),
            _resident((d, 3 * d)),
            _resident((CONV_W, d)),
            _resident((1, d)),
            _resident((d, d)),
        ],
        out_specs=(pl.BlockSpec((1, tm, d), lambda i, t: (i, t, 0)),
                   pl.BlockSpec((1, CONV_W - 1, d), lambda i, t: (i, 0, 0))),
        scratch_shapes=[pltpu.VMEM((tm + 8, d), F32)],
        compiler_params=_params("arbitrary", "arbitrary"),
        name="mixer_seq",
    )(x, state, g, w_in, conv_w, conv_b, w_out)


def _mixer_step_kernel(x_ref, s0_ref, s1_ref, g_ref, win_ref, cw_ref, cb_ref, wout_ref,
                       h_ref, u_ref, *, d):
    x = x_ref[...]
    hn = (_rms_hat(x) * g_ref[...]).astype(BF16)
    gc = _dot(hn, win_ref[:, d:2 * d])
    xin = _dot(hn, win_ref[:, 2 * d:3 * d])
    u = gc * xin
    conv = (cb_ref[...] + s0_ref[...] * cw_ref[0:1, :]
            + s1_ref[...] * cw_ref[1:2, :] + u * cw_ref[2:3, :])
    gb = _dot(hn, win_ref[:, 0:d])
    y = _dot((gb * conv).astype(BF16), wout_ref[...])
    h_ref[...] = x + y
    u_ref[...] = u


def _mixer_step(x, s0, s1, g, w_in, conv_w, conv_b, w_out):
    n, d = x.shape
    kern = functools.partial(_mixer_step_kernel, d=d)
    return pl.pallas_call(
        kern,
        out_shape=(jax.ShapeDtypeStruct((n, d), F32), jax.ShapeDtypeStruct((n, d), F32)),
        compiler_params=pltpu.CompilerParams(vmem_limit_bytes=V7X_VMEM_LIMIT_BYTES),
        name="mixer_step",
    )(x, s0, s1, g, w_in, conv_w, conv_b, w_out)


def _mlp_ple_kernel(*refs, has_attn, has_final, ff, ff_chunk):
    refs = list(refs)
    h_ref = refs.pop(0)
    if has_attn:
        o_ref = refs.pop(0)
        wo_ref = refs.pop(0)
    mg_ref, wup_ref, wdown_ref, pg_ref, wgate_ref, p_ref, wproj_ref = refs[:7]
    refs = refs[7:]
    if has_final:
        fg_ref = refs.pop(0)
    out_ref = refs.pop(0)

    h = h_ref[...]
    if has_attn:
        h = h + _dot(o_ref[...], wo_ref[...])
    m = (_rms_hat(h) * mg_ref[...]).astype(BF16)
    acc = h
    for c in range(ff // ff_chunk):
        lo = c * ff_chunk
        a = _dot(m, wup_ref[:, lo:lo + ff_chunk])
        a = jnp.square(jnp.maximum(a, 0.0)).astype(BF16)
        acc = acc + _dot(a, wdown_ref[lo:lo + ff_chunk, :])
    gate = jax.nn.sigmoid(_dot((_rms_hat(acc) * pg_ref[...]).astype(BF16), wgate_ref[...]))
    h3 = acc + gate * _dot(p_ref[...].astype(BF16), wproj_ref[...])
    if has_final:
        h3 = _rms_hat(h3) * fg_ref[...]
    out_ref[...] = h3


def _mlp_ple(h, p, mg, w_up, w_down, pg, w_gate, w_proj, attn=None, final_g=None):
    n, d = h.shape
    ff = w_up.shape[1]
    ple = p.shape[1]
    tm = min(TOKEN_TILE, n)
    assert n % tm == 0
    ff_chunk = min(FF_CHUNK, ff)
    assert ff % ff_chunk == 0
    row = lambda i: (i, 0)
    args = [h]
    specs = [pl.BlockSpec((tm, d), row)]
    if attn is not None:
        o, wo = attn
        args += [o, wo]
        specs += [pl.BlockSpec((tm, o.shape[1]), row), _resident(wo.shape)]
    args += [mg, w_up, w_down, pg, w_gate, p, w_proj]
    specs += [_resident((1, d)), _resident((d, ff)), _resident((ff, d)), _resident((1, d)),
              _resident((d, d)), pl.BlockSpec((tm, ple), row), _resident((ple, d))]
    if final_g is not None:
        args.append(final_g)
        specs.append(_resident((1, d)))
    kern = functools.partial(_mlp_ple_kernel, has_attn=attn is not None,
                             has_final=final_g is not None, ff=ff, ff_chunk=ff_chunk)
    return pl.pallas_call(
        kern,
        out_shape=jax.ShapeDtypeStruct((n, d), F32),
        grid=(n // tm,),
        in_specs=specs,
        out_specs=pl.BlockSpec((tm, d), row),
        compiler_params=_params("arbitrary"),
        name="mlp_ple",
    )(*args)


def _rope_tables(pos, head_dim):
    rope_dim = head_dim // 4
    half = rope_dim // 2
    inv = jnp.power(jnp.float32(ROPE_THETA),
                    -jnp.arange(0, rope_dim, 2, dtype=jnp.float32) / rope_dim)
    ang = pos.astype(jnp.float32)[:, None] * inv[None, :]
    cos = jnp.cos(ang)
    sin = jnp.sin(ang)
    dcol = jnp.arange(2 * head_dim) % head_dim
    j = dcol % half
    c = jnp.where(dcol[None, :] < rope_dim, cos[:, j], 1.0)
    sa = jnp.where(dcol[None, :] < half, -sin[:, j], 0.0)
    sb = jnp.where((dcol[None, :] >= half) & (dcol[None, :] < rope_dim), sin[:, j], 0.0)
    return c.astype(F32), sa.astype(F32), sb.astype(F32)


def _rope_head(t, c, sa, sb, half):
    width = t.shape[-1]
    return t * c + pltpu.roll(t, width - half, 1) * sa + pltpu.roll(t, half, 1) * sb


def _qkv_kernel(h_ref, kvg_ref, qg_ref, wk_ref, wv_ref, wq_ref, c_ref, sa_ref, sb_ref,
                k_ref, v_ref, qb_ref, *attn_refs, n_heads, half, q_scale):
    tm = h_ref.shape[1]
    h = h_ref[0]
    xh = _rms_hat(h)
    hk = (xh * kvg_ref[...]).astype(BF16)
    hq = (xh * qg_ref[...]).astype(BF16)
    c = c_ref[...]
    sa = sa_ref[...]
    sb = sb_ref[...]
    k = _dot(hk, wk_ref[...])
    q = _dot(hq, wq_ref[...])
    v = _dot(hk, wv_ref[...])
    v_ref[0] = v
    for hd in range(n_heads):
        cols = slice(hd * LANES, (hd + 1) * LANES)
        kh = _rope_head(k[:, cols], c, sa, sb, half)
        k_ref[0, :, cols] = kh
        qb_ref[0, :, cols] = (_rope_head(q[:, cols], c, sa, sb, half) * q_scale).astype(BF16)
        if attn_refs:
            attn_refs[0][0, :, cols] = kh.astype(BF16)
    if attn_refs:
        vt = v.T.astype(BF16)
        va = LANES + VT_ONES_ROWS
        for hd in range(n_heads):
            attn_refs[1][0, 0, hd * va:hd * va + LANES, :] = vt[hd * LANES:(hd + 1) * LANES, :]
            attn_refs[1][0, 0, hd * va + LANES:(hd + 1) * va, :] = jnp.ones((VT_ONES_ROWS, tm), BF16)


def _qkv(h, kvg, qg, w_k, w_v, w_q, tables, n_heads, head_dim, for_attention):
    b, s, d = h.shape
    tm = min(TOKEN_TILE, s)
    assert s % tm == 0 and 2 * head_dim == LANES
    nt = s // tm
    tok = lambda i, t: (i, t, 0)
    tab = pl.BlockSpec((tm, LANES), lambda i, t: (t, 0))
    out_shape = [jax.ShapeDtypeStruct((b, s, d), F32), jax.ShapeDtypeStruct((b, s, d), F32),
                 jax.ShapeDtypeStruct((b, s, d), BF16)]
    out_specs = [pl.BlockSpec((1, tm, d), tok)] * 3
    if for_attention:
        vt_rows = n_heads * (LANES + VT_ONES_ROWS)
        out_shape += [jax.ShapeDtypeStruct((b, s, d), BF16),
                      jax.ShapeDtypeStruct((b, nt, vt_rows, tm), BF16)]
        out_specs += [pl.BlockSpec((1, tm, d), tok),
                      pl.BlockSpec((1, 1, vt_rows, tm), lambda i, t: (i, t, 0, 0))]
    kern = functools.partial(_qkv_kernel, n_heads=n_heads, half=head_dim // 8,
                             q_scale=head_dim ** -0.5 * LOG2E)
    return pl.pallas_call(
        kern,
        out_shape=tuple(out_shape),
        grid=(b, nt),
        in_specs=[pl.BlockSpec((1, tm, d), tok), _resident((1, d)), _resident((1, d)),
                  _resident((d, d)), _resident((d, d)), _resident((d, d)), tab, tab, tab],
        out_specs=tuple(out_specs),
        compiler_params=_params("arbitrary", "arbitrary"),
        name="qkv_rope",
    )(h, kvg, qg, w_k, w_v, w_q, *tables)


def _lambda_value(lq1_ref, lk1_ref, lq2_ref, lk2_ref, lam_init):
    s1 = jnp.sum(lq1_ref[...] * lk1_ref[...], axis=-1, keepdims=True)
    s2 = jnp.sum(lq2_ref[...] * lk2_ref[...], axis=-1, keepdims=True)
    return jnp.exp(s1) - jnp.exp(s2) + lam_init


def _prompt_attn_kernel(lq1_ref, lk1_ref, lq2_ref, lk2_ref, sg_ref, q_ref, k_ref, vt_ref,
                        o_ref, m_ref, acc_ref, *, blk, head_dim, vdim, lam_init):
    qi = pl.program_id(2)
    q = q_ref[0]
    lane = lax.broadcasted_iota(jnp.int32, q.shape, 1)
    zero = jnp.zeros_like(q)
    q_comp = (jnp.where(lane < head_dim, q, zero), jnp.where(lane >= head_dim, q, zero))

    m_ref[...] = jnp.full(m_ref.shape, NEG, F32)
    acc_ref[...] = jnp.zeros(acc_ref.shape, F32)

    def scores(j):
        start = pl.multiple_of(j * blk, blk)
        kb = k_ref[0, pl.ds(start, blk), :]
        return tuple(_dot_nt(kb, qc) for qc in q_comp)

    def update(j, s_pair, masked):
        vt = vt_ref[0, j]
        for c in range(2):
            s = s_pair[c]
            if masked:
                kpos = lax.broadcasted_iota(jnp.int32, s.shape, 0)
                qpos = lax.broadcasted_iota(jnp.int32, s.shape, 1)
                s = jnp.where(kpos <= qpos, s, NEG)
            m_old = m_ref[c]
            m_new = jnp.maximum(m_old, jnp.max(s, axis=0, keepdims=True))
            alpha = jnp.exp2(m_old - m_new)
            p = jnp.exp2(s - m_new).astype(BF16)
            acc_ref[c] = alpha * acc_ref[c] + _dot(vt, p)
            m_ref[c] = m_new

    def body(j, s_pair):
        s_next = scores(j + 1)
        update(j, s_pair, False)
        return s_next

    s_last = lax.fori_loop(0, qi, body, scores(0))
    update(qi, s_last, True)

    lam = _lambda_value(lq1_ref, lk1_ref, lq2_ref, lk2_ref, lam_init)
    o1 = acc_ref[0, 0:vdim, :] / acc_ref[0, vdim:vdim + 1, :]
    o2 = acc_ref[1, 0:vdim, :] / acc_ref[1, vdim:vdim + 1, :]
    out = o1 - lam * o2
    ms = jnp.mean(out * out, axis=0, keepdims=True)
    y = out * lax.rsqrt(ms + EPS) * sg_ref[...] * (1.0 - lam_init)
    o_ref[0] = y.T.astype(BF16)


def _prompt_attention(q, k, vt, lam_vecs, sub_g, n_heads, head_dim, lam_init):
    b, s, d = q.shape
    blk = vt.shape[-1]
    nq = s // blk
    vdim = d // n_heads
    va = vdim + VT_ONES_ROWS
    assert vdim == LANES and 2 * head_dim == LANES and vt.shape[2] == n_heads * va
    vec = _resident((1, head_dim))
    kern = functools.partial(_prompt_attn_kernel, blk=blk, head_dim=head_dim, vdim=vdim,
                             lam_init=lam_init)
    return pl.pallas_call(
        kern,
        out_shape=jax.ShapeDtypeStruct((b, s, d), BF16),
        grid=(b, n_heads, nq),
        in_specs=[vec, vec, vec, vec, _resident((vdim, 1)),
                  pl.BlockSpec((1, blk, LANES), lambda i, h, t: (i, t, h)),
                  pl.BlockSpec((1, s, LANES), lambda i, h, t: (i, 0, h)),
                  pl.BlockSpec((1, nq, va, blk), lambda i, h, t: (i, 0, h, 0))],
        out_specs=pl.BlockSpec((1, blk, LANES), lambda i, h, t: (i, t, h)),
        scratch_shapes=[pltpu.VMEM((2, 1, blk), F32), pltpu.VMEM((2, va, blk), F32)],
        compiler_params=_params("arbitrary", "arbitrary", "arbitrary"),
        name="prompt_diff_attn",
    )(*lam_vecs, sub_g, q, k, vt)


def _decode_attn_kernel(pt_ref, lq1_ref, lk1_ref, lq2_ref, lk2_ref, sg_ref,
                        q_ref, kn_ref, vn_ref, e_ref, *rest, pages, n_heads, head_dim, lam_init):
    k_refs = rest[:pages]
    v_refs = rest[pages:2 * pages]
    o_ref, m_ref, l_ref, acc_ref = rest[2 * pages:]
    del pt_ref
    j = pl.program_id(1)
    rows = 2 * n_heads
    d = q_ref.shape[-1]
    page = k_refs[0].shape[-1]

    qrow = q_ref[0].astype(F32)
    row = lax.broadcasted_iota(jnp.int32, (rows, d), 0)
    col = lax.broadcasted_iota(jnp.int32, (rows, d), 1)
    own = (col // head_dim) == 2 * (row % n_heads) + row // n_heads
    qm32 = jnp.where(own, jnp.broadcast_to(qrow, (rows, d)), 0.0)
    qm = qm32.astype(BF16)

    @pl.when(j == 0)
    def _():
        m_ref[...] = jnp.sum(qm32 * kn_ref[0], axis=1, keepdims=True)
        l_ref[...] = jnp.ones(l_ref.shape, F32)
        acc_ref[...] = jnp.concatenate([vn_ref[0], vn_ref[0]], axis=0)

    s = jnp.concatenate([_dot(qm, k_refs[g][0].astype(BF16)) for g in range(pages)], axis=1)
    m_old = m_ref[...]
    m_new = jnp.maximum(m_old, jnp.max(s, axis=1, keepdims=True))
    alpha = jnp.exp2(m_old - m_new)
    p = jnp.exp2(s - m_new).astype(BF16)
    l_ref[...] = alpha * l_ref[...] + jnp.sum(p.astype(F32), axis=1, keepdims=True)
    m_ref[...] = m_new
    p_stack = jnp.concatenate([p[:, g * page:(g + 1) * page] for g in range(pages)], axis=0)
    spread = _dot(p_stack, e_ref[...])
    srow = lax.broadcasted_iota(jnp.int32, spread.shape, 0)
    scol = lax.broadcasted_iota(jnp.int32, spread.shape, 1)
    p_exp = jnp.where(scol % n_heads == srow % n_heads, spread, 0.0).astype(BF16)
    pv = _dot(p_exp[0:rows], v_refs[0][0].astype(BF16))
    for g in range(1, pages):
        pv = pv + _dot(p_exp[g * rows:(g + 1) * rows], v_refs[g][0].astype(BF16))
    acc_ref[...] = alpha * acc_ref[...] + pv

    @pl.when(j == pl.num_programs(1) - 1)
    def _():
        lam = _lambda_value(lq1_ref, lk1_ref, lq2_ref, lk2_ref, lam_init)
        w = acc_ref[...] / l_ref[...]
        out = w[0:n_heads] - lam * w[n_heads:rows]
        o_ref[0] = (_rms_hat(out) * sg_ref[...] * (1.0 - lam_init)).astype(BF16)


def _decode_attention(q, k_new, v_new, cache_k, cache_v, page_table, lam_vecs, sub_g,
                      n_heads, head_dim, lam_init):
    n, d = q.shape
    n_pool, page = cache_k.shape[:2]
    vdim = cache_v.shape[-1]
    n_pages = page_table.shape[1]
    pages = math.gcd(DECODE_PAGES_PER_STEP, n_pages)
    rows = 2 * n_heads
    kt = jnp.transpose(cache_k, (0, 2, 3, 4, 1)).reshape(n_pool, d, page)
    v2 = cache_v.reshape(n_pool, page * n_heads, vdim)
    spread = (jnp.arange(page * n_heads)[None, :] // n_heads
              == jnp.arange(page)[:, None]).astype(BF16)

    def k_spec(g):
        return pl.BlockSpec((1, d, page), lambda i, j, pt: (pt[i, j * pages + g], 0, 0))

    def v_spec(g):
        return pl.BlockSpec((1, page * n_heads, vdim),
                            lambda i, j, pt: (pt[i, j * pages + g], 0, 0))

    const = lambda shape: pl.BlockSpec(shape, lambda i, j, pt: (0,) * len(shape))
    per_seq = pl.BlockSpec((1, 1, d), lambda i, j, pt: (i, 0, 0))
    per_seq_heads = pl.BlockSpec((1, n_heads, vdim), lambda i, j, pt: (i, 0, 0))
    kern = functools.partial(_decode_attn_kernel, pages=pages, n_heads=n_heads,
                             head_dim=head_dim, lam_init=lam_init)
    grid_spec = pltpu.PrefetchScalarGridSpec(
        num_scalar_prefetch=1,
        grid=(n, n_pages // pages),
        in_specs=[const((1, head_dim))] * 4 + [const((1, vdim))]
                 + [per_seq, per_seq, per_seq_heads, const((page, page * n_heads))]
                 + [k_spec(g) for g in range(pages)] + [v_spec(g) for g in range(pages)],
        out_specs=per_seq_heads,
        scratch_shapes=[pltpu.VMEM((rows, 1), F32), pltpu.VMEM((rows, 1), F32),
                        pltpu.VMEM((rows, vdim), F32)],
    )
    out = pl.pallas_call(
        kern,
        out_shape=jax.ShapeDtypeStruct((n, n_heads, vdim), BF16),
        grid_spec=grid_spec,
        compiler_params=_params("arbitrary", "arbitrary"),
        name="decode_diff_attn",
    )(page_table, *lam_vecs, sub_g, q.reshape(n, 1, d), k_new.reshape(n, 1, d),
      v_new.reshape(n, n_heads, vdim), spread, *([kt] * pages), *([v2] * pages))
    return out.reshape(n, d)


def kernel(x_prompt, x_sample, p_prompt, p_sample, state_conv, cache_k, cache_v, page_table,
           a_norm, a_w_in, a_conv_w, a_conv_b, a_w_out, kv_norm, w_k, w_v, b_norm, b_wq,
           b_lq1, b_lk1, b_lq2, b_lk2, b_subln, b_wo, mlp_norm, w_up, w_down, ple_norm,
           w_ple_gate, w_ple_proj, final_norm):
    batch, seq, d = x_prompt.shape
    dec_batch, dec_seq, _ = x_sample.shape
    depth = p_prompt.shape[0]
    n_pool, page, n_heads, two, head_dim = cache_k.shape
    n_a = state_conv.shape[0]
    assert depth == 2 and n_a == 1 and two == 2 and dec_seq == 1
    past_len = page_table.shape[1] * page
    li_b = n_a
    lam_init = _lambda_init(li_b)

    bf = lambda w: w.astype(BF16)
    row = lambda g: g.reshape(1, -1).astype(F32)
    w_in, w_out = bf(a_w_in[0]), bf(a_w_out[0])
    wk, wv, wq, wo = bf(w_k), bf(w_v), bf(b_wq[0]), bf(b_wo[0])
    wup, wdown = bf(w_up), bf(w_down)
    wgate, wproj = bf(w_ple_gate), bf(w_ple_proj)
    lam_vecs = tuple(row(v[0]) for v in (b_lq1, b_lk1, b_lq2, b_lk2))
    mlp_args = lambda li: (row(mlp_norm[li]), wup[li], wdown[li], row(ple_norm[li]),
                           wgate[li], wproj[li])

    zeros_state = jnp.zeros((batch, CONV_W - 1, d), F32)
    h, conv_p = _mixer_seq(x_prompt, zeros_state, row(a_norm[0]), w_in, a_conv_w[0],
                           row(a_conv_b[0]), w_out)
    h = _mlp_ple(h.reshape(batch * seq, d), p_prompt[0].reshape(batch * seq, -1), *mlp_args(0))
    tables_p = _rope_tables(jnp.arange(seq), head_dim)
    k_p, v_p, q_b, k_b, vt_b = _qkv(h.reshape(batch, seq, d), row(kv_norm), row(b_norm[0]),
                                    wk, wv, wq, tables_p, n_heads, head_dim, True)
    o = _prompt_attention(q_b, k_b, vt_b, lam_vecs, b_subln[0].reshape(-1, 1).astype(F32),
                          n_heads, head_dim, lam_init)
    y_prompt = _mlp_ple(h, p_prompt[1].reshape(batch * seq, -1), *mlp_args(1),
                        attn=(o.reshape(batch * seq, d), wo), final_g=row(final_norm))

    xs = x_sample.reshape(dec_batch, d)
    hs, u_s = _mixer_step(xs, state_conv[0, :, 0], state_conv[0, :, 1], row(a_norm[0]), w_in,
                          a_conv_w[0], row(a_conv_b[0]), w_out)
    conv_s = jnp.stack([state_conv[0, :, 1], u_s], axis=1)
    hs = _mlp_ple(hs, p_sample[0].reshape(dec_batch, -1), *mlp_args(0))
    tables_s = _rope_tables(jnp.full((dec_batch,), past_len, jnp.int32), head_dim)
    k_s, v_s, q_s = _qkv(hs.reshape(1, dec_batch, d), row(kv_norm), row(b_norm[0]),
                         wk, wv, wq, tables_s, n_heads, head_dim, False)
    k_s, v_s, q_s = (a.reshape(dec_batch, d) for a in (k_s, v_s, q_s))
    o_s = _decode_attention(q_s, k_s, v_s, cache_k, cache_v, page_table, lam_vecs,
                            row(b_subln[0]), n_heads, head_dim, lam_init)
    y_sample = _mlp_ple(hs, p_sample[1].reshape(dec_batch, -1), *mlp_args(1),
                        attn=(o_s, wo), final_g=row(final_norm))

    return (y_prompt.reshape(batch, seq, d),
            y_sample.reshape(dec_batch, dec_seq, d),
            conv_p[None],
            conv_s[None],
            k_p.reshape(batch, seq, n_heads, 2, head_dim),
            v_p.reshape(batch, seq, n_heads, d // n_heads),
            k_s.reshape(dec_batch, dec_seq, n_heads, 2, head_dim),
            v_s.reshape(dec_batch, dec_seq, n_heads, d // n_heads))
```

```python
import functools
import math

import jax
import jax.numpy as jnp
from jax import lax
from jax.experimental import pallas as pl
from jax.experimental.pallas import tpu as pltpu

F32 = jnp.float32
BF16 = jnp.bfloat16

EPS = 1e-6
NEG = -1e30
ROPE_THETA = 500000.0
CONV_W = 3

V7X_VMEM_LIMIT_BYTES = 56 * 1024 * 1024
LANES = 128

TOKEN_TILE = 512
FF_CHUNK = 1024
VT_ONES_ROWS = 16
LOG2E = math.log2(math.e)
DECODE_PAGES_PER_STEP = 8


def _lambda_init(layer_idx):
    return 0.8 - 0.6 * math.exp(-0.3 * layer_idx)


def _rms_hat(x):
    return x * lax.rsqrt(jnp.mean(x * x, axis=-1, keepdims=True) + EPS)


def _dot(a, b):
    return jnp.dot(a, b, preferred_element_type=F32)


def _dot_nt(a, b):
    return lax.dot_general(a, b, (((1,), (1,)), ((), ())), preferred_element_type=F32)


def _resident(shape):
    nd = len(shape)
    return pl.BlockSpec(shape, lambda *_: (0,) * nd, pipeline_mode=pl.Buffered(1))


def _params(*sem):
    return pltpu.CompilerParams(dimension_semantics=sem, vmem_limit_bytes=V7X_VMEM_LIMIT_BYTES)


def _mixer_seq_kernel(x_ref, st_ref, g_ref, win_ref, cw_ref, cb_ref, wout_ref,
                      h_ref, stout_ref, ubuf, *, tm, d):
    @pl.when(pl.program_id(1) == 0)
    def _():
        ubuf[6:8, :] = st_ref[0]

    x = x_ref[0]
    hn = (_rms_hat(x) * g_ref[...]).astype(BF16)
    gc = _dot(hn, win_ref[:, d:2 * d])
    xin = _dot(hn, win_ref[:, 2 * d:3 * d])
    u = gc * xin
    ubuf[8:8 + tm, :] = u
    conv = (cb_ref[...] + ubuf[6:6 + tm, :] * cw_ref[0:1, :]
            + ubuf[7:7 + tm, :] * cw_ref[1:2, :] + u * cw_ref[2:3, :])
    gb = _dot(hn, win_ref[:, 0:d])
    y = _dot((gb * conv).astype(BF16), wout_ref[...])
    h_ref[0] = x + y
    last = ubuf[6 + tm:8 + tm, :]
    stout_ref[0] = last
    ubuf[6:8, :] = last


def _mixer_seq(x, state, g, w_in, conv_w, conv_b, w_out):
    b, s, d = x.shape
    tm = min(TOKEN_TILE, s)
    assert s % tm == 0 and tm >= CONV_W - 1
    kern = functools.partial(_mixer_seq_kernel, tm=tm, d=d)
    return pl.pallas_call(
        kern,
        out_shape=(jax.ShapeDtypeStruct((b, s, d), F32),
                   jax.ShapeDtypeStruct((b, CONV_W - 1, d), F32)),
        grid=(b, s // tm),
        in_specs=[
            pl.BlockSpec((1, tm, d), lambda i, t: (i, t, 0)),
            pl.BlockSpec((1, CONV_W - 1, d), lambda i, t: (i, 0, 0)),
            _resident((1, d)),
            _resident((d, 3 * d)),
            _resident((CONV_W, d)),
            _resident((1, d)),
            _resident((d, d)),
        ],
        out_specs=(pl.BlockSpec((1, tm, d), lambda i, t: (i, t, 0)),
                   pl.BlockSpec((1, CONV_W - 1, d), lambda i, t: (i, 0, 0))),
        scratch_shapes=[pltpu.VMEM((tm + 8, d), F32)],
        compiler_params=_params("arbitrary", "arbitrary"),
        name="mixer_seq",
    )(x, state, g, w_in, conv_w, conv_b, w_out)


def _mixer_step_kernel(x_ref, s0_ref, s1_ref, g_ref, win_ref, cw_ref, cb_ref, wout_ref,
                       h_ref, u_ref, *, d):
    x = x_ref[...]
    hn = (_rms_hat(x) * g_ref[...]).astype(BF16)
    gc = _dot(hn, win_ref[:, d:2 * d])
    xin = _dot(hn, win_ref[:, 2 * d:3 * d])
    u = gc * xin
    conv = (cb_ref[...] + s0_ref[...] * cw_ref[0:1, :]
            + s1_ref[...] * cw_ref[1:2, :] + u * cw_ref[2:3, :])
    gb = _dot(hn, win_ref[:, 0:d])
    y = _dot((gb * conv).astype(BF16), wout_ref[...])
    h_ref[...] = x + y
    u_ref[...] = u


def _mixer_step(x, s0, s1, g, w_in, conv_w, conv_b, w_out):
    n, d = x.shape
    kern = functools.partial(_mixer_step_kernel, d=d)
    return pl.pallas_call(
        kern,
        out_shape=(jax.ShapeDtypeStruct((n, d), F32), jax.ShapeDtypeStruct((n, d), F32)),
        compiler_params=pltpu.CompilerParams(vmem_limit_bytes=V7X_VMEM_LIMIT_BYTES),
        name="mixer_step",
    )(x, s0, s1, g, w_in, conv_w, conv_b, w_out)


def _mlp_ple_kernel(*refs, has_attn, has_final, ff, ff_chunk):
    refs = list(refs)
    h_ref = refs.pop(0)
    if has_attn:
        o_ref = refs.pop(0)
        wo_ref = refs.pop(0)
    mg_ref, wup_ref, wdown_ref, pg_ref, wgate_ref, p_ref, wproj_ref = refs[:7]
    refs = refs[7:]
    if has_final:
        fg_ref = refs.pop(0)
    out_ref = refs.pop(0)

    h = h_ref[...]
    if has_attn:
        h = h + _dot(o_ref[...], wo_ref[...])
    m = (_rms_hat(h) * mg_ref[...]).astype(BF16)
    acc = h
    for c in range(ff // ff_chunk):
        lo = c * ff_chunk
        a = _dot(m, wup_ref[:, lo:lo + ff_chunk])
        a = jnp.square(jnp.maximum(a, 0.0)).astype(BF16)
        acc = acc + _dot(a, wdown_ref[lo:lo + ff_chunk, :])
    gate = jax.nn.sigmoid(_dot((_rms_hat(acc) * pg_ref[...]).astype(BF16), wgate_ref[...]))
    h3 = acc + gate * _dot(p_ref[...].astype(BF16), wproj_ref[...])
    if has_final:
        h3 = _rms_hat(h3) * fg_ref[...]
    out_ref[...] = h3


def _mlp_ple(h, p, mg, w_up, w_down, pg, w_gate, w_proj, attn=None, final_g=None):
    n, d = h.shape
    ff = w_up.shape[1]
    ple = p.shape[1]
    tm = min(TOKEN_TILE, n)
    assert n % tm == 0
    ff_chunk = min(FF_CHUNK, ff)
    assert ff % ff_chunk == 0
    row = lambda i: (i, 0)
    args = [h]
    specs = [pl.BlockSpec((tm, d), row)]
    if attn is not None:
        o, wo = attn
        args += [o, wo]
        specs += [pl.BlockSpec((tm, o.shape[1]), row), _resident(wo.shape)]
    args += [mg, w_up, w_down, pg, w_gate, p, w_proj]
    specs += [_resident((1, d)), _resident((d, ff)), _resident((ff, d)), _resident((1, d)),
              _resident((d, d)), pl.BlockSpec((tm, ple), row), _resident((ple, d))]
    if final_g is not None:
        args.append(final_g)
        specs.append(_resident((1, d)))
    kern = functools.partial(_mlp_ple_kernel, has_attn=attn is not None,
                             has_final=final_g is not None, ff=ff, ff_chunk=ff_chunk)
    return pl.pallas_call(
        kern,
        out_shape=jax.ShapeDtypeStruct((n, d), F32),
        grid=(n // tm,),
        in_specs=specs,
        out_specs=pl.BlockSpec((tm, d), row),
        compiler_params=_params("arbitrary"),
        name="mlp_ple",
    )(*args)


def _rope_tables(pos, head_dim):
    rope_dim = head_dim // 4
    half = rope_dim // 2
    inv = jnp.power(jnp.float32(ROPE_THETA),
                    -jnp.arange(0, rope_dim, 2, dtype=jnp.float32) / rope_dim)
    ang = pos.astype(jnp.float32)[:, None] * inv[None, :]
    cos = jnp.cos(ang)
    sin = jnp.sin(ang)
    dcol = jnp.arange(2 * head_dim) % head_dim
    j = dcol % half
    c = jnp.where(dcol[None, :] < rope_dim, cos[:, j], 1.0)
    sa = jnp.where(dcol[None, :] < half, -sin[:, j], 0.0)
    sb = jnp.where((dcol[None, :] >= half) & (dcol[None, :] < rope_dim), sin[:, j], 0.0)
    return c.astype(F32), sa.astype(F32), sb.astype(F32)


def _rope_head(t, c, sa, sb, half):
    width = t.shape[-1]
    return t * c + pltpu.roll(t, width - half, 1) * sa + pltpu.roll(t, half, 1) * sb


def _qkv_kernel(h_ref, kvg_ref, qg_ref, wk_ref, wv_ref, wq_ref, c_ref, sa_ref, sb_ref,
                k_ref, v_ref, qb_ref, *attn_refs, n_heads, half, q_scale):
    tm = h_ref.shape[1]
    h = h_ref[0]
    xh = _rms_hat(h)
    hk = (xh * kvg_ref[...]).astype(BF16)
    hq = (xh * qg_ref[...]).astype(BF16)
    c = c_ref[...]
    sa = sa_ref[...]
    sb = sb_ref[...]
    k = _dot(hk, wk_ref[...])
    q = _dot(hq, wq_ref[...])
    v = _dot(hk, wv_ref[...])
    v_ref[0] = v
    for hd in range(n_heads):
        cols = slice(hd * LANES, (hd + 1) * LANES)
        kh = _rope_head(k[:, cols], c, sa, sb, half)
        k_ref[0, :, cols] = kh
        qb_ref[0, :, cols] = (_rope_head(q[:, cols], c, sa, sb, half) * q_scale).astype(BF16)
        if attn_refs:
            attn_refs[0][0, :, cols] = kh.astype(BF16)
    if attn_refs:
        vt = v.T.astype(BF16)
        va = LANES + VT_ONES_ROWS
        for hd in range(n_heads):
            attn_refs[1][0, 0, hd * va:hd * va + LANES, :] = vt[hd * LANES:(hd + 1) * LANES, :]
            attn_refs[1][0, 0, hd * va + LANES:(hd + 1) * va, :] = jnp.ones((VT_ONES_ROWS, tm), BF16)


def _qkv(h, kvg, qg, w_k, w_v, w_q, tables, n_heads, head_dim, for_attention):
    b, s, d = h.shape
    tm = min(TOKEN_TILE, s)
    assert s % tm == 0 and 2 * head_dim == LANES
    nt = s // tm
    tok = lambda i, t: (i, t, 0)
    tab = pl.BlockSpec((tm, LANES), lambda i, t: (t, 0))
    out_shape = [jax.ShapeDtypeStruct((b, s, d), F32), jax.ShapeDtypeStruct((b, s, d), F32),
                 jax.ShapeDtypeStruct((b, s, d), BF16)]
    out_specs = [pl.BlockSpec((1, tm, d), tok)] * 3
    if for_attention:
        vt_rows = n_heads * (LANES + VT_ONES_ROWS)
        out_shape += [jax.ShapeDtypeStruct((b, s, d), BF16),
                      jax.ShapeDtypeStruct((b, nt, vt_rows, tm), BF16)]
        out_specs += [pl.BlockSpec((1, tm, d), tok),
                      pl.BlockSpec((1, 1, vt_rows, tm), lambda i, t: (i, t, 0, 0))]
    kern = functools.partial(_qkv_kernel, n_heads=n_heads, half=head_dim // 8,
                             q_scale=head_dim ** -0.5 * LOG2E)
    return pl.pallas_call(
        kern,
        out_shape=tuple(out_shape),
        grid=(b, nt),
        in_specs=[pl.BlockSpec((1, tm, d), tok), _resident((1, d)), _resident((1, d)),
                  _resident((d, d)), _resident((d, d)), _resident((d, d)), tab, tab, tab],
        out_specs=tuple(out_specs),
        compiler_params=_params("arbitrary", "arbitrary"),
        name="qkv_rope",
    )(h, kvg, qg, w_k, w_v, w_q, *tables)


def _lambda_value(lq1_ref, lk1_ref, lq2_ref, lk2_ref, lam_init):
    s1 = jnp.sum(lq1_ref[...] * lk1_ref[...], axis=-1, keepdims=True)
    s2 = jnp.sum(lq2_ref[...] * lk2_ref[...], axis=-1, keepdims=True)
    return jnp.exp(s1) - jnp.exp(s2) + lam_init


def _prompt_attn_kernel(lq1_ref, lk1_ref, lq2_ref, lk2_ref, sg_ref, q_ref, k_ref, vt_ref,
                        o_ref, m_ref, acc_ref, sa_ref, sb_ref, *, blk, head_dim, vdim, lam_init):
    qi = pl.program_id(2)
    q = q_ref[0]
    lane = lax.broadcasted_iota(jnp.int32, q.shape, 1)
    zero = jnp.zeros_like(q)
    q_comp = (jnp.where(lane < head_dim, q, zero), jnp.where(lane >= head_dim, q, zero))

    m_ref[...] = jnp.full(m_ref.shape, NEG, F32)
    acc_ref[...] = jnp.zeros(acc_ref.shape, F32)

    def scores(j, s_ref):
        start = pl.multiple_of(j * blk, blk)
        kb = k_ref[0, pl.ds(start, blk), :]
        for c in range(2):
            s_ref[c] = _dot_nt(kb, q_comp[c])

    def update(j, s_ref, masked):
        vt = vt_ref[0, j]
        for c in range(2):
            s = s_ref[c]
            if masked:
                kpos = lax.broadcasted_iota(jnp.int32, s.shape, 0)
                qpos = lax.broadcasted_iota(jnp.int32, s.shape, 1)
                s = jnp.where(kpos <= qpos, s, NEG)
            m_old = m_ref[c]
            m_new = jnp.maximum(m_old, jnp.max(s, axis=0, keepdims=True))
            alpha = jnp.exp2(m_old - m_new)
            p = jnp.exp2(s - m_new).astype(BF16)
            acc_ref[c] = alpha * acc_ref[c] + _dot(vt, p)
            m_ref[c] = m_new

    scores(0, sa_ref)

    def body(i, carry):
        j = 2 * i
        scores(j + 1, sb_ref)
        update(j, sa_ref, False)
        scores(j + 2, sa_ref)
        update(j + 1, sb_ref, False)
        return carry

    lax.fori_loop(0, qi // 2, body, 0)

    @pl.when(qi % 2 == 1)
    def _():
        scores(qi, sb_ref)
        update(qi - 1, sa_ref, False)
        update(qi, sb_ref, True)

    @pl.when(qi % 2 == 0)
    def _():
        update(qi, sa_ref, True)

    lam = _lambda_value(lq1_ref, lk1_ref, lq2_ref, lk2_ref, lam_init)
    o1 = acc_ref[0, 0:vdim, :] / acc_ref[0, vdim:vdim + 1, :]
    o2 = acc_ref[1, 0:vdim, :] / acc_ref[1, vdim:vdim + 1, :]
    out = o1 - lam * o2
    ms = jnp.mean(out * out, axis=0, keepdims=True)
    y = out * lax.rsqrt(ms + EPS) * sg_ref[...] * (1.0 - lam_init)
    o_ref[0] = y.T.astype(BF16)


def _prompt_attention(q, k, vt, lam_vecs, sub_g, n_heads, head_dim, lam_init):
    b, s, d = q.shape
    blk = vt.shape[-1]
    nq = s // blk
    vdim = d // n_heads
    va = vdim + VT_ONES_ROWS
    assert vdim == LANES and 2 * head_dim == LANES and vt.shape[2] == n_heads * va
    vec = _resident((1, head_dim))
    kern = functools.partial(_prompt_attn_kernel, blk=blk, head_dim=head_dim, vdim=vdim,
                             lam_init=lam_init)
    return pl.pallas_call(
        kern,
        out_shape=jax.ShapeDtypeStruct((b, s, d), BF16),
        grid=(b, n_heads, nq),
        in_specs=[vec, vec, vec, vec, _resident((vdim, 1)),
                  pl.BlockSpec((1, blk, LANES), lambda i, h, t: (i, t, h)),
                  pl.BlockSpec((1, s, LANES), lambda i, h, t: (i, 0, h)),
                  pl.BlockSpec((1, nq, va, blk), lambda i, h, t: (i, 0, h, 0))],
        out_specs=pl.BlockSpec((1, blk, LANES), lambda i, h, t: (i, t, h)),
        scratch_shapes=[pltpu.VMEM((2, 1, blk), F32), pltpu.VMEM((2, va, blk), F32),
                        pltpu.VMEM((2, blk, blk), F32), pltpu.VMEM((2, blk, blk), F32)],
        compiler_params=_params("arbitrary", "arbitrary", "arbitrary"),
        name="prompt_diff_attn",
    )(*lam_vecs, sub_g, q, k, vt)


def _decode_attn_kernel(pt_ref, lq1_ref, lk1_ref, lq2_ref, lk2_ref, sg_ref,
                        q_ref, kn_ref, vn_ref, e_ref, *rest, pages, n_heads, head_dim, lam_init):
    k_refs = rest[:pages]
    v_refs = rest[pages:2 * pages]
    o_ref, m_ref, l_ref, acc_ref = rest[2 * pages:]
    del pt_ref
    j = pl.program_id(1)
    rows = 2 * n_heads
    d = q_ref.shape[-1]
    page = k_refs[0].shape[-1]

    qrow = q_ref[0].astype(F32)
    row = lax.broadcasted_iota(jnp.int32, (rows, d), 0)
    col = lax.broadcasted_iota(jnp.int32, (rows, d), 1)
    own = (col // head_dim) == 2 * (row % n_heads) + row // n_heads
    qm32 = jnp.where(own, jnp.broadcast_to(qrow, (rows, d)), 0.0)
    qm = qm32.astype(BF16)

    @pl.when(j == 0)
    def _():
        m_ref[...] = jnp.sum(qm32 * kn_ref[0], axis=1, keepdims=True)
        l_ref[...] = jnp.ones(l_ref.shape, F32)
        acc_ref[...] = jnp.concatenate([vn_ref[0], vn_ref[0]], axis=0)

    s = jnp.concatenate([_dot(qm, k_refs[g][0].astype(BF16)) for g in range(pages)], axis=1)
    m_old = m_ref[...]
    m_new = jnp.maximum(m_old, jnp.max(s, axis=1, keepdims=True))
    alpha = jnp.exp2(m_old - m_new)
    p = jnp.exp2(s - m_new).astype(BF16)
    l_ref[...] = alpha * l_ref[...] + jnp.sum(p.astype(F32), axis=1, keepdims=True)
    m_ref[...] = m_new
    p_stack = jnp.concatenate([p[:, g * page:(g + 1) * page] for g in range(pages)], axis=0)
    spread = _dot(p_stack, e_ref[...])
    srow = lax.broadcasted_iota(jnp.int32, spread.shape, 0)
    scol = lax.broadcasted_iota(jnp.int32, spread.shape, 1)
    p_exp = jnp.where(scol % n_heads == srow % n_heads, spread, 0.0).astype(BF16)
    pv = _dot(p_exp[0:rows], v_refs[0][0].astype(BF16))
    for g in range(1, pages):
        pv = pv + _dot(p_exp[g * rows:(g + 1) * rows], v_refs[g][0].astype(BF16))
    acc_ref[...] = alpha * acc_ref[...] + pv

    @pl.when(j == pl.num_programs(1) - 1)
    def _():
        lam = _lambda_value(lq1_ref, lk1_ref, lq2_ref, lk2_ref, lam_init)
        w = acc_ref[...] / l_ref[...]
        out = w[0:n_heads] - lam * w[n_heads:rows]
        o_ref[0] = (_rms_hat(out) * sg_ref[...] * (1.0 - lam_init)).astype(BF16)


def _decode_attention(q, k_new, v_new, cache_k, cache_v, page_table, lam_vecs, sub_g,
                      n_heads, head_dim, lam_init):
    n, d = q.shape
    n_pool, page = cache_k.shape[:2]
    vdim = cache_v.shape[-1]
    n_pages = page_table.shape[1]
    pages = math.gcd(DECODE_PAGES_PER_STEP, n_pages)
    rows = 2 * n_heads
    kt = jnp.transpose(cache_k, (0, 2, 3, 4, 1)).reshape(n_pool, d, page)
    v2 = cache_v.reshape(n_pool, page * n_heads, vdim)
    spread = (jnp.arange(page * n_heads)[None, :] // n_heads
              == jnp.arange(page)[:, None]).astype(BF16)

    def k_spec(g):
        return pl.BlockSpec((1, d, page), lambda i, j, pt: (pt[i, j * pages + g], 0, 0))

    def v_spec(g):
        return pl.BlockSpec((1, page * n_heads, vdim),
                            lambda i, j, pt: (pt[i, j * pages + g], 0, 0))

    const = lambda shape: pl.BlockSpec(shape, lambda i, j, pt: (0,) * len(shape))
    per_seq = pl.BlockSpec((1, 1, d), lambda i, j, pt: (i, 0, 0))
    per_seq_heads = pl.BlockSpec((1, n_heads, vdim), lambda i, j, pt: (i, 0, 0))
    kern = functools.partial(_decode_attn_kernel, pages=pages, n_heads=n_heads,
                             head_dim=head_dim, lam_init=lam_init)
    grid_spec = pltpu.PrefetchScalarGridSpec(
        num_scalar_prefetch=1,
        grid=(n, n_pages // pages),
        in_specs=[const((1, head_dim))] * 4 + [const((1, vdim))]
                 + [per_seq, per_seq, per_seq_heads, const((page, page * n_heads))]
                 + [k_spec(g) for g in range(pages)] + [v_spec(g) for g in range(pages)],
        out_specs=per_seq_heads,
        scratch_shapes=[pltpu.VMEM((rows, 1), F32), pltpu.VMEM((rows, 1), F32),
                        pltpu.VMEM((rows, vdim), F32)],
    )
    out = pl.pallas_call(
        kern,
        out_shape=jax.ShapeDtypeStruct((n, n_heads, vdim), BF16),
        grid_spec=grid_spec,
        compiler_params=_params("arbitrary", "arbitrary"),
        name="decode_diff_attn",
    )(page_table, *lam_vecs, sub_g, q.reshape(n, 1, d), k_new.reshape(n, 1, d),
      v_new.reshape(n, n_heads, vdim), spread, *([kt] * pages), *([v2] * pages))
    return out.reshape(n, d)


def kernel(x_prompt, x_sample, p_prompt, p_sample, state_conv, cache_k, cache_v, page_table,
           a_norm, a_w_in, a_conv_w, a_conv_b, a_w_out, kv_norm, w_k, w_v, b_norm, b_wq,
           b_lq1, b_lk1, b_lq2, b_lk2, b_subln, b_wo, mlp_norm, w_up, w_down, ple_norm,
           w_ple_gate, w_ple_proj, final_norm):
    batch, seq, d = x_prompt.shape
    dec_batch, dec_seq, _ = x_sample.shape
    depth = p_prompt.shape[0]
    n_pool, page, n_heads, two, head_dim = cache_k.shape
    n_a = state_conv.shape[0]
    assert depth == 2 and n_a == 1 and two == 2 and dec_seq == 1
    past_len = page_table.shape[1] * page
    li_b = n_a
    lam_init = _lambda_init(li_b)

    bf = lambda w: w.astype(BF16)
    row = lambda g: g.reshape(1, -1).astype(F32)
    w_in, w_out = bf(a_w_in[0]), bf(a_w_out[0])
    wk, wv, wq, wo = bf(w_k), bf(w_v), bf(b_wq[0]), bf(b_wo[0])
    wup, wdown = bf(w_up), bf(w_down)
    wgate, wproj = bf(w_ple_gate), bf(w_ple_proj)
    lam_vecs = tuple(row(v[0]) for v in (b_lq1, b_lk1, b_lq2, b_lk2))
    mlp_args = lambda li: (row(mlp_norm[li]), wup[li], wdown[li], row(ple_norm[li]),
                           wgate[li], wproj[li])

    zeros_state = jnp.zeros((batch, CONV_W - 1, d), F32)
    h, conv_p = _mixer_seq(x_prompt, zeros_state, row(a_norm[0]), w_in, a_conv_w[0],
                           row(a_conv_b[0]), w_out)
    h = _mlp_ple(h.reshape(batch * seq, d), p_prompt[0].reshape(batch * seq, -1), *mlp_args(0))
    tables_p = _rope_tables(jnp.arange(seq), head_dim)
    k_p, v_p, q_b, k_b, vt_b = _qkv(h.reshape(batch, seq, d), row(kv_norm), row(b_norm[0]),
                                    wk, wv, wq, tables_p, n_heads, head_dim, True)
    o = _prompt_attention(q_b, k_b, vt_b, lam_vecs, b_subln[0].reshape(-1, 1).astype(F32),
                          n_heads, head_dim, lam_init)
    y_prompt = _mlp_ple(h, p_prompt[1].reshape(batch * seq, -1), *mlp_args(1),
                        attn=(o.reshape(batch * seq, d), wo), final_g=row(final_norm))

    xs = x_sample.reshape(dec_batch, d)
    hs, u_s = _mixer_step(xs, state_conv[0, :, 0], state_conv[0, :, 1], row(a_norm[0]), w_in,
                          a_conv_w[0], row(a_conv_b[0]), w_out)
    conv_s = jnp.stack([state_conv[0, :, 1], u_s], axis=1)
    hs = _mlp_ple(hs, p_sample[0].reshape(dec_batch, -1), *mlp_args(0))
    tables_s = _rope_tables(jnp.full((dec_batch,), past_len, jnp.int32), head_dim)
    k_s, v_s, q_s = _qkv(hs.reshape(1, dec_batch, d), row(kv_norm), row(b_norm[0]),
                         wk, wv, wq, tables_s, n_heads, head_dim, False)
    k_s, v_s, q_s = (a.reshape(dec_batch, d) for a in (k_s, v_s, q_s))
    o_s = _decode_attention(q_s, k_s, v_s, cache_k, cache_v, page_table, lam_vecs,
                            row(b_subln[0]), n_heads, head_dim, lam_init)
    y_sample = _mlp_ple(hs, p_sample[1].reshape(dec_batch, -1), *mlp_args(1),
                        attn=(o_s, wo), final_g=row(final_norm))

    return (y_prompt.reshape(batch, seq, d),
            y_sample.reshape(dec_batch, dec_seq, d),
            conv_p[None],
            conv_s[None],
            k_p.reshape(batch, seq, n_heads, 2, head_dim),
            v_p.reshape(batch, seq, n_heads, d // n_heads),
            k_s.reshape(dec_batch, dec_seq, n_heads, 2, head_dim),
            v_s.reshape(dec_batch, dec_seq, n_heads, d // n_heads))
```

```python
import functools
import math

import jax
import jax.numpy as jnp
from jax import lax
from jax.experimental import pallas as pl
from jax.experimental.pallas import tpu as pltpu

F32 = jnp.float32
BF16 = jnp.bfloat16

EPS = 1e-6
NEG = -1e30
ROPE_THETA = 500000.0
CONV_W = 3

V7X_VMEM_LIMIT_BYTES = 56 * 1024 * 1024
LANES = 128

TOKEN_TILE = 512
FF_CHUNK = 1024
VT_ONES_ROWS = 16
LOG2E = math.log2(math.e)


def _lambda_init(layer_idx):
    return 0.8 - 0.6 * math.exp(-0.3 * layer_idx)


def _rms_hat(x):
    return x * lax.rsqrt(jnp.mean(x * x, axis=-1, keepdims=True) + EPS)


def _dot(a, b):
    return jnp.dot(a, b, preferred_element_type=F32)


def _dot_nt(a, b):
    return lax.dot_general(a, b, (((1,), (1,)), ((), ())), preferred_element_type=F32)


def _resident(shape):
    nd = len(shape)
    return pl.BlockSpec(shape, lambda *_: (0,) * nd, pipeline_mode=pl.Buffered(1))


def _params(*sem):
    return pltpu.CompilerParams(dimension_semantics=sem, vmem_limit_bytes=V7X_VMEM_LIMIT_BYTES)


def _mixer_seq_kernel(x_ref, st_ref, g_ref, win_ref, cw_ref, cb_ref, wout_ref,
                      h_ref, stout_ref, ubuf, *, tm, d):
    @pl.when(pl.program_id(1) == 0)
    def _():
        ubuf[6:8, :] = st_ref[0]

    x = x_ref[0]
    hn = (_rms_hat(x) * g_ref[...]).astype(BF16)
    gc = _dot(hn, win_ref[:, d:2 * d])
    xin = _dot(hn, win_ref[:, 2 * d:3 * d])
    u = gc * xin
    ubuf[8:8 + tm, :] = u
    conv = (cb_ref[...] + ubuf[6:6 + tm, :] * cw_ref[0:1, :]
            + ubuf[7:7 + tm, :] * cw_ref[1:2, :] + u * cw_ref[2:3, :])
    gb = _dot(hn, win_ref[:, 0:d])
    y = _dot((gb * conv).astype(BF16), wout_ref[...])
    h_ref[0] = x + y
    last = ubuf[6 + tm:8 + tm, :]
    stout_ref[0] = last
    ubuf[6:8, :] = last


def _mixer_seq(x, state, g, w_in, conv_w, conv_b, w_out):
    b, s, d = x.shape
    tm = min(TOKEN_TILE, s)
    assert s % tm == 0 and tm >= CONV_W - 1
    kern = functools.partial(_mixer_seq_kernel, tm=tm, d=d)
    return pl.pallas_call(
        kern,
        out_shape=(jax.ShapeDtypeStruct((b, s, d), F32),
                   jax.ShapeDtypeStruct((b, CONV_W - 1, d), F32)),
        grid=(b, s // tm),
        in_specs=[
            pl.BlockSpec((1, tm, d), lambda i, t: (i, t, 0)),
            pl.BlockSpec((1, CONV_W - 1, d), lambda i, t: (i, 0, 0)),
            _resident((1, d)),
            _resident((d, 3 * d)),
            _resident((CONV_W, d)),
            _resident((1, d)),
            _resident((d, d)),
        ],
        out_specs=(pl.BlockSpec((1, tm, d), lambda i, t: (i, t, 0)),
                   pl.BlockSpec((1, CONV_W - 1, d), lambda i, t: (i, 0, 0))),
        scratch_shapes=[pltpu.VMEM((tm + 8, d), F32)],
        compiler_params=_params("arbitrary", "arbitrary"),
        name="mixer_seq",
    )(x, state, g, w_in, conv_w, conv_b, w_out)


def _mixer_step_kernel(x_ref, s0_ref, s1_ref, g_ref, win_ref, cw_ref, cb_ref, wout_ref,
                       h_ref, u_ref, *, d):
    x = x_ref[...]
    hn = (_rms_hat(x) * g_ref[...]).astype(BF16)
    gc = _dot(hn, win_ref[:, d:2 * d])
    xin = _dot(hn, win_ref[:, 2 * d:3 * d])
    u = gc * xin
    conv = (cb_ref[...] + s0_ref[...] * cw_ref[0:1, :]
            + s1_ref[...] * cw_ref[1:2, :] + u * cw_ref[2:3, :])
    gb = _dot(hn, win_ref[:, 0:d])
    y = _dot((gb * conv).astype(BF16), wout_ref[...])
    h_ref[...] = x + y
    u_ref[...] = u


def _mixer_step(x, s0, s1, g, w_in, conv_w, conv_b, w_out):
    n, d = x.shape
    kern = functools.partial(_mixer_step_kernel, d=d)
    return pl.pallas_call(
        kern,
        out_shape=(jax.ShapeDtypeStruct((n, d), F32), jax.ShapeDtypeStruct((n, d), F32)),
        compiler_params=pltpu.CompilerParams(vmem_limit_bytes=V7X_VMEM_LIMIT_BYTES),
        name="mixer_step",
    )(x, s0, s1, g, w_in, conv_w, conv_b, w_out)


def _mlp_ple_kernel(*refs, has_attn, has_final, ff, ff_chunk):
    refs = list(refs)
    h_ref = refs.pop(0)
    if has_attn:
        o_ref = refs.pop(0)
        wo_ref = refs.pop(0)
    mg_ref, wup_ref, wdown_ref, pg_ref, wgate_ref, p_ref, wproj_ref = refs[:7]
    refs = refs[7:]
    if has_final:
        fg_ref = refs.pop(0)
    out_ref = refs.pop(0)

    h = h_ref[...]
    if has_attn:
        h = h + _dot(o_ref[...], wo_ref[...])
    m = (_rms_hat(h) * mg_ref[...]).astype(BF16)
    acc = h
    for c in range(ff // ff_chunk):
        lo = c * ff_chunk
        a = _dot(m, wup_ref[:, lo:lo + ff_chunk])
        a = jnp.square(jnp.maximum(a, 0.0)).astype(BF16)
        acc = acc + _dot(a, wdown_ref[lo:lo + ff_chunk, :])
    gate = jax.nn.sigmoid(_dot((_rms_hat(acc) * pg_ref[...]).astype(BF16), wgate_ref[...]))
    h3 = acc + gate * _dot(p_ref[...].astype(BF16), wproj_ref[...])
    if has_final:
        h3 = _rms_hat(h3) * fg_ref[...]
    out_ref[...] = h3


def _mlp_ple(h, p, mg, w_up, w_down, pg, w_gate, w_proj, attn=None, final_g=None):
    n, d = h.shape
    ff = w_up.shape[1]
    ple = p.shape[1]
    tm = min(TOKEN_TILE, n)
    assert n % tm == 0
    ff_chunk = min(FF_CHUNK, ff)
    assert ff % ff_chunk == 0
    row = lambda i: (i, 0)
    args = [h]
    specs = [pl.BlockSpec((tm, d), row)]
    if attn is not None:
        o, wo = attn
        args += [o, wo]
        specs += [pl.BlockSpec((tm, o.shape[1]), row), _resident(wo.shape)]
    args += [mg, w_up, w_down, pg, w_gate, p, w_proj]
    specs += [_resident((1, d)), _resident((d, ff)), _resident((ff, d)), _resident((1, d)),
              _resident((d, d)), pl.BlockSpec((tm, ple), row), _resident((ple, d))]
    if final_g is not None:
        args.append(final_g)
        specs.append(_resident((1, d)))
    kern = functools.partial(_mlp_ple_kernel, has_attn=attn is not None,
                             has_final=final_g is not None, ff=ff, ff_chunk=ff_chunk)
    return pl.pallas_call(
        kern,
        out_shape=jax.ShapeDtypeStruct((n, d), F32),
        grid=(n // tm,),
        in_specs=specs,
        out_specs=pl.BlockSpec((tm, d), row),
        compiler_params=_params("arbitrary"),
        name="mlp_ple",
    )(*args)


def _rope_tables(pos, head_dim):
    rope_dim = head_dim // 4
    half = rope_dim // 2
    inv = jnp.power(jnp.float32(ROPE_THETA),
                    -jnp.arange(0, rope_dim, 2, dtype=jnp.float32) / rope_dim)
    ang = pos.astype(jnp.float32)[:, None] * inv[None, :]
    cos = jnp.cos(ang)
    sin = jnp.sin(ang)
    dcol = jnp.arange(2 * head_dim) % head_dim
    j = dcol % half
    c = jnp.where(dcol[None, :] < rope_dim, cos[:, j], 1.0)
    sa = jnp.where(dcol[None, :] < half, -sin[:, j], 0.0)
    sb = jnp.where((dcol[None, :] >= half) & (dcol[None, :] < rope_dim), sin[:, j], 0.0)
    return c.astype(F32), sa.astype(F32), sb.astype(F32)


def _rope_head(t, c, sa, sb, half):
    width = t.shape[-1]
    return t * c + pltpu.roll(t, width - half, 1) * sa + pltpu.roll(t, half, 1) * sb


def _qkv_kernel(h_ref, kvg_ref, qg_ref, wk_ref, wv_ref, wq_ref, c_ref, sa_ref, sb_ref,
                k_ref, v_ref, qb_ref, *attn_refs, n_heads, half, q_scale):
    tm = h_ref.shape[1]
    h = h_ref[0]
    xh = _rms_hat(h)
    hk = (xh * kvg_ref[...]).astype(BF16)
    hq = (xh * qg_ref[...]).astype(BF16)
    c = c_ref[...]
    sa = sa_ref[...]
    sb = sb_ref[...]
    k = _dot(hk, wk_ref[...])
    q = _dot(hq, wq_ref[...])
    v = _dot(hk, wv_ref[...])
    v_ref[0] = v
    for hd in range(n_heads):
        cols = slice(hd * LANES, (hd + 1) * LANES)
        kh = _rope_head(k[:, cols], c, sa, sb, half)
        k_ref[0, :, cols] = kh
        qb_ref[0, :, cols] = (_rope_head(q[:, cols], c, sa, sb, half) * q_scale).astype(BF16)
        if attn_refs:
            attn_refs[0][0, :, cols] = kh.astype(BF16)
    if attn_refs:
        vt = v.T.astype(BF16)
        va = LANES + VT_ONES_ROWS
        for hd in range(n_heads):
            attn_refs[1][0, 0, hd * va:hd * va + LANES, :] = vt[hd * LANES:(hd + 1) * LANES, :]
            attn_refs[1][0, 0, hd * va + LANES:(hd + 1) * va, :] = jnp.ones((VT_ONES_ROWS, tm), BF16)


def _qkv(h, kvg, qg, w_k, w_v, w_q, tables, n_heads, head_dim, for_attention):
    b, s, d = h.shape
    tm = min(TOKEN_TILE, s)
    assert s % tm == 0 and 2 * head_dim == LANES
    nt = s // tm
    tok = lambda i, t: (i, t, 0)
    tab = pl.BlockSpec((tm, LANES), lambda i, t: (t, 0))
    out_shape = [jax.ShapeDtypeStruct((b, s, d), F32), jax.ShapeDtypeStruct((b, s, d), F32),
                 jax.ShapeDtypeStruct((b, s, d), BF16)]
    out_specs = [pl.BlockSpec((1, tm, d), tok)] * 3
    if for_attention:
        vt_rows = n_heads * (LANES + VT_ONES_ROWS)
        out_shape += [jax.ShapeDtypeStruct((b, s, d), BF16),
                      jax.ShapeDtypeStruct((b, nt, vt_rows, tm), BF16)]
        out_specs += [pl.BlockSpec((1, tm, d), tok),
                      pl.BlockSpec((1, 1, vt_rows, tm), lambda i, t: (i, t, 0, 0))]
    kern = functools.partial(_qkv_kernel, n_heads=n_heads, half=head_dim // 8,
                             q_scale=head_dim ** -0.5 * LOG2E)
    return pl.pallas_call(
        kern,
        out_shape=tuple(out_shape),
        grid=(b, nt),
        in_specs=[pl.BlockSpec((1, tm, d), tok), _resident((1, d)), _resident((1, d)),
                  _resident((d, d)), _resident((d, d)), _resident((d, d)), tab, tab, tab],
        out_specs=tuple(out_specs),
        compiler_params=_params("arbitrary", "arbitrary"),
        name="qkv_rope",
    )(h, kvg, qg, w_k, w_v, w_q, *tables)


def _lambda_value(lq1_ref, lk1_ref, lq2_ref, lk2_ref, lam_init):
    s1 = jnp.sum(lq1_ref[...] * lk1_ref[...], axis=-1, keepdims=True)
    s2 = jnp.sum(lq2_ref[...] * lk2_ref[...], axis=-1, keepdims=True)
    return jnp.exp(s1) - jnp.exp(s2) + lam_init


def _prompt_block(lam, sg_ref, q_ref, k_ref, vt_ref, o_ref, m_ref, acc_ref, sa_ref, sb_ref,
                  *, qi, blk, head_dim, vdim, lam_init, alongside_first_scores):
    q = q_ref[0]
    lane = lax.broadcasted_iota(jnp.int32, q.shape, 1)
    zero = jnp.zeros_like(q)
    q_comp = (jnp.where(lane < head_dim, q, zero), jnp.where(lane >= head_dim, q, zero))

    m_ref[...] = jnp.full(m_ref.shape, NEG, F32)
    acc_ref[...] = jnp.zeros(acc_ref.shape, F32)

    def scores(j, s_ref):
        start = pl.multiple_of(j * blk, blk)
        kb = k_ref[0, pl.ds(start, blk), :]
        for c in range(2):
            s_ref[c] = _dot_nt(kb, q_comp[c])

    def update(j, s_ref, masked):
        vt = vt_ref[0, j]
        for c in range(2):
            s = s_ref[c]
            if masked:
                kpos = lax.broadcasted_iota(jnp.int32, s.shape, 0)
                qpos = lax.broadcasted_iota(jnp.int32, s.shape, 1)
                s = jnp.where(kpos <= qpos, s, NEG)
            m_old = m_ref[c]
            m_new = jnp.maximum(m_old, jnp.max(s, axis=0, keepdims=True))
            alpha = jnp.exp2(m_old - m_new)
            p = jnp.exp2(s - m_new).astype(BF16)
            acc_ref[c] = alpha * acc_ref[c] + _dot(vt, p)
            m_ref[c] = m_new

    scores(0, sa_ref)
    alongside_first_scores()

    def body(i, carry):
        j = 2 * i
        scores(j + 1, sb_ref)
        update(j, sa_ref, False)
        scores(j + 2, sa_ref)
        update(j + 1, sb_ref, False)
        return carry

    lax.fori_loop(0, qi // 2, body, 0)

    @pl.when(qi % 2 == 1)
    def _():
        scores(qi, sb_ref)
        update(qi - 1, sa_ref, False)
        update(qi, sb_ref, True)

    @pl.when(qi % 2 == 0)
    def _():
        update(qi, sa_ref, True)

    o1 = acc_ref[0, 0:vdim, :] / acc_ref[0, vdim:vdim + 1, :]
    o2 = acc_ref[1, 0:vdim, :] / acc_ref[1, vdim:vdim + 1, :]
    out = o1 - lam * o2
    ms = jnp.mean(out * out, axis=0, keepdims=True)
    y = out * lax.rsqrt(ms + EPS) * sg_ref[...] * (1.0 - lam_init)
    o_ref[0] = y.T.astype(BF16)


def _decode_start(q_ref, kn_ref, vn_ref, m_ref, l_ref, acc_ref, *, n_heads, head_dim):
    rows = 2 * n_heads
    d = q_ref.shape[-1]
    qrow = q_ref[0].astype(F32)
    row = lax.broadcasted_iota(jnp.int32, (rows, d), 0)
    col = lax.broadcasted_iota(jnp.int32, (rows, d), 1)
    own = (col // head_dim) == 2 * (row % n_heads) + row // n_heads
    qm = jnp.where(own, jnp.broadcast_to(qrow, (rows, d)), 0.0)
    m_ref[...] = jnp.sum(qm * kn_ref[0], axis=1, keepdims=True)
    l_ref[...] = jnp.ones(l_ref.shape, F32)
    acc_ref[...] = jnp.concatenate([vn_ref[0], vn_ref[0]], axis=0)


def _decode_finish(lam, sg_ref, o_ref, l_ref, acc_ref, *, n_heads, lam_init):
    w = acc_ref[...] / l_ref[...]
    out = w[0:n_heads] - lam * w[n_heads:2 * n_heads]
    o_ref[0] = (_rms_hat(out) * sg_ref[...] * (1.0 - lam_init)).astype(BF16)


def _decode_pages(qb_ref, e_ref, k_refs, v_refs, m_ref, l_ref, acc_ref, *, n_heads, head_dim):
    pages = len(k_refs)
    rows = 2 * n_heads
    page = k_refs[0].shape[-1]

    qb = qb_ref[0]
    sub = 8
    srow = lax.broadcasted_iota(jnp.int32, (rows, page * n_heads), 0)
    scol = lax.broadcasted_iota(jnp.int32, (rows, page * n_heads), 1)
    own_head = scol % n_heads == srow % n_heads

    def page_scores(g):
        prod = k_refs[g][0] * qb
        parts = []
        for r in range(rows):
            base = (2 * (r % n_heads) + r // n_heads) * head_dim
            part = prod[base:base + sub]
            for u in range(1, head_dim // sub):
                part = part + prod[base + u * sub:base + (u + 1) * sub]
            parts.append(part)
        stacked = jnp.concatenate(parts, axis=0)
        return jnp.sum(stacked.reshape(rows, sub, page), axis=1)

    halves = 2 if pages % 2 == 0 else 1
    per = pages // halves
    for hf in range(halves):
        group = range(hf * per, (hf + 1) * per)
        s = jnp.concatenate([page_scores(g) for g in group], axis=1)
        m_old = m_ref[...]
        m_new = jnp.maximum(m_old, jnp.max(s, axis=1, keepdims=True))
        alpha = jnp.exp2(m_old - m_new)
        p = jnp.exp2(s - m_new).astype(BF16)
        l_ref[...] = alpha * l_ref[...] + jnp.sum(p.astype(F32), axis=1, keepdims=True)
        m_ref[...] = m_new
        p_stack = jnp.concatenate([p[:, u * page:(u + 1) * page] for u in range(per)], axis=0)
        spread = _dot(p_stack, e_ref[...])
        pv = None
        for u, g in enumerate(group):
            p_exp = jnp.where(own_head, spread[u * rows:(u + 1) * rows], 0.0).astype(BF16)
            term = _dot(p_exp, v_refs[g][0].astype(BF16))
            pv = term if pv is None else pv + term
        acc_ref[...] = alpha * acc_ref[...] + pv


def _attention_kernel(pt_ref, lq1_ref, lk1_ref, lq2_ref, lk2_ref, sgc_ref, sgr_ref,
                      q_ref, k_ref, vt_ref, dq_ref, dqb_ref, kn_ref, vn_ref, e_ref, *rest,
                      pages, n_dsteps, blk, n_heads, head_dim, vdim, lam_init):
    del pt_ref
    k_refs = rest[:pages]
    v_refs = rest[pages:2 * pages]
    o_ref, od_ref, m_ref, acc_ref, sa_ref, sb_ref, dm_ref, dl_ref, dacc_ref = rest[2 * pages:]
    lam = _lambda_value(lq1_ref, lk1_ref, lq2_ref, lk2_ref, lam_init)
    step = (pl.program_id(0) * pl.num_programs(1) + pl.program_id(1)) * pl.num_programs(2) \
        + pl.program_id(2)
    dstep = step % n_dsteps

    @pl.when(dstep == 0)
    def _():
        _decode_start(dq_ref, kn_ref, vn_ref, dm_ref, dl_ref, dacc_ref,
                      n_heads=n_heads, head_dim=head_dim)

    decode = functools.partial(_decode_pages, dqb_ref, e_ref, k_refs, v_refs, dm_ref, dl_ref,
                               dacc_ref, n_heads=n_heads, head_dim=head_dim)
    _prompt_block(lam, sgc_ref, q_ref, k_ref, vt_ref, o_ref, m_ref, acc_ref, sa_ref, sb_ref,
                  qi=pl.program_id(2), blk=blk, head_dim=head_dim, vdim=vdim, lam_init=lam_init,
                  alongside_first_scores=decode)

    @pl.when(dstep == n_dsteps - 1)
    def _():
        _decode_finish(lam, sgr_ref, od_ref, dl_ref, dacc_ref, n_heads=n_heads, lam_init=lam_init)


def _attention(q, k, vt, q_s, k_new, v_new, cache_k, cache_v, page_table, lam_vecs, sub_g,
               n_heads, head_dim, lam_init):
    b, s, d = q.shape
    blk = vt.shape[-1]
    nq = s // blk
    vdim = d // n_heads
    va = vdim + VT_ONES_ROWS
    assert vdim == LANES and 2 * head_dim == LANES and vt.shape[2] == n_heads * va
    n = q_s.shape[0]
    n_pool, page = cache_k.shape[:2]
    n_pages = page_table.shape[1]
    steps = b * n_heads * nq
    assert steps % n == 0 and n_pages % (steps // n) == 0
    n_dsteps = steps // n
    pages = n_pages // n_dsteps
    rows = 2 * n_heads
    kt = jnp.transpose(cache_k, (0, 2, 3, 4, 1)).reshape(n_pool, d, page)
    v2 = cache_v.reshape(n_pool, page * n_heads, vdim)
    spread = (jnp.arange(page * n_heads)[None, :] // n_heads
              == jnp.arange(page)[:, None]).astype(BF16)
    q_lanes = jnp.broadcast_to(q_s.astype(F32)[:, :, None], (n, d, page))

    def lin(i, h, t):
        return (i * n_heads + h) * nq + t

    def seq_of(i, h, t):
        return lin(i, h, t) // n_dsteps

    def page_of(g):
        return lambda i, h, t, pt: (pt[seq_of(i, h, t), (lin(i, h, t) % n_dsteps) * pages + g], 0, 0)

    const = lambda shape: pl.BlockSpec(shape, lambda i, h, t, pt: (0,) * len(shape))
    per_seq = lambda shape: pl.BlockSpec((1,) + shape, lambda i, h, t, pt: (seq_of(i, h, t), 0, 0))
    kern = functools.partial(_attention_kernel, pages=pages, n_dsteps=n_dsteps, blk=blk,
                             n_heads=n_heads, head_dim=head_dim, vdim=vdim, lam_init=lam_init)
    grid_spec = pltpu.PrefetchScalarGridSpec(
        num_scalar_prefetch=1,
        grid=(b, n_heads, nq),
        in_specs=[const((1, head_dim))] * 4 + [const((vdim, 1)), const((1, vdim)),
                  pl.BlockSpec((1, blk, LANES), lambda i, h, t, pt: (i, t, h)),
                  pl.BlockSpec((1, s, LANES), lambda i, h, t, pt: (i, 0, h)),
                  pl.BlockSpec((1, nq, va, blk), lambda i, h, t, pt: (i, 0, h, 0)),
                  per_seq((1, d)), per_seq((d, page)), per_seq((1, d)), per_seq((n_heads, vdim)),
                  const((page, page * n_heads))]
                 + [pl.BlockSpec((1, d, page), page_of(g)) for g in range(pages)]
                 + [pl.BlockSpec((1, page * n_heads, vdim), page_of(g)) for g in range(pages)],
        out_specs=(pl.BlockSpec((1, blk, LANES), lambda i, h, t, pt: (i, t, h)),
                   per_seq((n_heads, vdim))),
        scratch_shapes=[pltpu.VMEM((2, 1, blk), F32), pltpu.VMEM((2, va, blk), F32),
                        pltpu.VMEM((2, blk, blk), F32), pltpu.VMEM((2, blk, blk), F32),
                        pltpu.VMEM((rows, 1), F32), pltpu.VMEM((rows, 1), F32),
                        pltpu.VMEM((rows, vdim), F32)],
    )
    o, o_s = pl.pallas_call(
        kern,
        out_shape=(jax.ShapeDtypeStruct((b, s, d), BF16),
                   jax.ShapeDtypeStruct((n, n_heads, vdim), BF16)),
        grid_spec=grid_spec,
        compiler_params=_params("arbitrary", "arbitrary", "arbitrary"),
        name="diff_attention",
    )(page_table, *lam_vecs, sub_g.reshape(-1, 1), sub_g.reshape(1, -1), q, k, vt,
      q_s.reshape(n, 1, d), q_lanes, k_new.reshape(n, 1, d), v_new.reshape(n, n_heads, vdim),
      spread, *([kt] * pages), *([v2] * pages))
    return o, o_s.reshape(n, d)


def kernel(x_prompt, x_sample, p_prompt, p_sample, state_conv, cache_k, cache_v, page_table,
           a_norm, a_w_in, a_conv_w, a_conv_b, a_w_out, kv_norm, w_k, w_v, b_norm, b_wq,
           b_lq1, b_lk1, b_lq2, b_lk2, b_subln, b_wo, mlp_norm, w_up, w_down, ple_norm,
           w_ple_gate, w_ple_proj, final_norm):
    batch, seq, d = x_prompt.shape
    dec_batch, dec_seq, _ = x_sample.shape
    depth = p_prompt.shape[0]
    n_pool, page, n_heads, two, head_dim = cache_k.shape
    n_a = state_conv.shape[0]
    assert depth == 2 and n_a == 1 and two == 2 and dec_seq == 1
    past_len = page_table.shape[1] * page
    li_b = n_a
    lam_init = _lambda_init(li_b)

    bf = lambda w: w.astype(BF16)
    row = lambda g: g.reshape(1, -1).astype(F32)
    w_in, w_out = bf(a_w_in[0]), bf(a_w_out[0])
    wk, wv, wq, wo = bf(w_k), bf(w_v), bf(b_wq[0]), bf(b_wo[0])
    wup, wdown = bf(w_up), bf(w_down)
    wgate, wproj = bf(w_ple_gate), bf(w_ple_proj)
    lam_vecs = tuple(row(v[0]) for v in (b_lq1, b_lk1, b_lq2, b_lk2))
    mlp_args = lambda li: (row(mlp_norm[li]), wup[li], wdown[li], row(ple_norm[li]),
                           wgate[li], wproj[li])

    zeros_state = jnp.zeros((batch, CONV_W - 1, d), F32)
    h, conv_p = _mixer_seq(x_prompt, zeros_state, row(a_norm[0]), w_in, a_conv_w[0],
                           row(a_conv_b[0]), w_out)
    h = _mlp_ple(h.reshape(batch * seq, d), p_prompt[0].reshape(batch * seq, -1), *mlp_args(0))
    tables_p = _rope_tables(jnp.arange(seq), head_dim)
    k_p, v_p, q_b, k_b, vt_b = _qkv(h.reshape(batch, seq, d), row(kv_norm), row(b_norm[0]),
                                    wk, wv, wq, tables_p, n_heads, head_dim, True)

    xs = x_sample.reshape(dec_batch, d)
    hs, u_s = _mixer_step(xs, state_conv[0, :, 0], state_conv[0, :, 1], row(a_norm[0]), w_in,
                          a_conv_w[0], row(a_conv_b[0]), w_out)
    conv_s = jnp.stack([state_conv[0, :, 1], u_s], axis=1)
    hs = _mlp_ple(hs, p_sample[0].reshape(dec_batch, -1), *mlp_args(0))
    tables_s = _rope_tables(jnp.full((dec_batch,), past_len, jnp.int32), head_dim)
    k_s, v_s, q_s = _qkv(hs.reshape(1, dec_batch, d), row(kv_norm), row(b_norm[0]),
                         wk, wv, wq, tables_s, n_heads, head_dim, False)
    k_s, v_s, q_s = (a.reshape(dec_batch, d) for a in (k_s, v_s, q_s))

    o, o_s = _attention(q_b, k_b, vt_b, q_s, k_s, v_s, cache_k, cache_v, page_table, lam_vecs,
                        b_subln[0].astype(F32), n_heads, head_dim, lam_init)
    y_prompt = _mlp_ple(h, p_prompt[1].reshape(batch * seq, -1), *mlp_args(1),
                        attn=(o.reshape(batch * seq, d), wo), final_g=row(final_norm))
    y_sample = _mlp_ple(hs, p_sample[1].reshape(dec_batch, -1), *mlp_args(1),
                        attn=(o_s, wo), final_g=row(final_norm))

    return (y_prompt.reshape(batch, seq, d),
            y_sample.reshape(dec_batch, dec_seq, d),
            conv_p[None],
            conv_s[None],
            k_p.reshape(batch, seq, n_heads, 2, head_dim),
            v_p.reshape(batch, seq, n_heads, d // n_heads),
            k_s.reshape(dec_batch, dec_seq, n_heads, 2, head_dim),
            v_s.reshape(dec_batch, dec_seq, n_heads, d // n_heads))
```

```python
import functools
import math

import jax
import jax.numpy as jnp
from jax import lax
from jax.experimental import pallas as pl
from jax.experimental.pallas import tpu as pltpu

F32 = jnp.float32
BF16 = jnp.bfloat16

EPS = 1e-6
NEG = -1e30
ROPE_THETA = 500000.0
CONV_W = 3

V7X_VMEM_LIMIT_BYTES = 56 * 1024 * 1024
LANES = 128

TOKEN_TILE = 512
FF_CHUNK = 1024
VT_ONES_ROWS = 16
LOG2E = math.log2(math.e)


def _lambda_init(layer_idx):
    return 0.8 - 0.6 * math.exp(-0.3 * layer_idx)


def _rms_hat(x):
    return x * lax.rsqrt(jnp.mean(x * x, axis=-1, keepdims=True) + EPS)


def _dot(a, b):
    return jnp.dot(a, b, preferred_element_type=F32)


def _dot_nt(a, b):
    return lax.dot_general(a, b, (((1,), (1,)), ((), ())), preferred_element_type=F32)


def _resident(shape):
    nd = len(shape)
    return pl.BlockSpec(shape, lambda *_: (0,) * nd, pipeline_mode=pl.Buffered(1))


def _params(*sem):
    return pltpu.CompilerParams(dimension_semantics=sem, vmem_limit_bytes=V7X_VMEM_LIMIT_BYTES)


def _mixer_seq_kernel(x_ref, st_ref, g_ref, win_ref, cw_ref, cb_ref, wout_ref,
                      h_ref, stout_ref, ubuf, *, tm, d):
    @pl.when(pl.program_id(1) == 0)
    def _():
        ubuf[6:8, :] = st_ref[0]

    x = x_ref[0]
    hn = (_rms_hat(x) * g_ref[...]).astype(BF16)
    gc = _dot(hn, win_ref[:, d:2 * d])
    xin = _dot(hn, win_ref[:, 2 * d:3 * d])
    u = gc * xin
    ubuf[8:8 + tm, :] = u
    conv = (cb_ref[...] + ubuf[6:6 + tm, :] * cw_ref[0:1, :]
            + ubuf[7:7 + tm, :] * cw_ref[1:2, :] + u * cw_ref[2:3, :])
    gb = _dot(hn, win_ref[:, 0:d])
    y = _dot((gb * conv).astype(BF16), wout_ref[...])
    h_ref[0] = x + y
    last = ubuf[6 + tm:8 + tm, :]
    stout_ref[0] = last
    ubuf[6:8, :] = last


def _mixer_seq(x, state, g, w_in, conv_w, conv_b, w_out):
    b, s, d = x.shape
    tm = min(TOKEN_TILE, s)
    assert s % tm == 0 and tm >= CONV_W - 1
    kern = functools.partial(_mixer_seq_kernel, tm=tm, d=d)
    return pl.pallas_call(
        kern,
        out_shape=(jax.ShapeDtypeStruct((b, s, d), F32),
                   jax.ShapeDtypeStruct((b, CONV_W - 1, d), F32)),
        grid=(b, s // tm),
        in_specs=[
            pl.BlockSpec((1, tm, d), lambda i, t: (i, t, 0)),
            pl.BlockSpec((1, CONV_W - 1, d), lambda i, t: (i, 0, 0)),
            _resident((1, d)),
            _resident((d, 3 * d)),
            _resident((CONV_W, d)),
            _resident((1, d)),
            _resident((d, d)),
        ],
        out_specs=(pl.BlockSpec((1, tm, d), lambda i, t: (i, t, 0)),
                   pl.BlockSpec((1, CONV_W - 1, d), lambda i, t: (i, 0, 0))),
        scratch_shapes=[pltpu.VMEM((tm + 8, d), F32)],
        compiler_params=_params("arbitrary", "arbitrary"),
        name="mixer_seq",
    )(x, state, g, w_in, conv_w, conv_b, w_out)


def _mixer_step_kernel(x_ref, s0_ref, s1_ref, g_ref, win_ref, cw_ref, cb_ref, wout_ref,
                       h_ref, u_ref, *, d):
    x = x_ref[...]
    hn = (_rms_hat(x) * g_ref[...]).astype(BF16)
    gc = _dot(hn, win_ref[:, d:2 * d])
    xin = _dot(hn, win_ref[:, 2 * d:3 * d])
    u = gc * xin
    conv = (cb_ref[...] + s0_ref[...] * cw_ref[0:1, :]
            + s1_ref[...] * cw_ref[1:2, :] + u * cw_ref[2:3, :])
    gb = _dot(hn, win_ref[:, 0:d])
    y = _dot((gb * conv).astype(BF16), wout_ref[...])
    h_ref[...] = x + y
    u_ref[...] = u


def _mixer_step(x, s0, s1, g, w_in, conv_w, conv_b, w_out):
    n, d = x.shape
    kern = functools.partial(_mixer_step_kernel, d=d)
    return pl.pallas_call(
        kern,
        out_shape=(jax.ShapeDtypeStruct((n, d), F32), jax.ShapeDtypeStruct((n, d), F32)),
        compiler_params=pltpu.CompilerParams(vmem_limit_bytes=V7X_VMEM_LIMIT_BYTES),
        name="mixer_step",
    )(x, s0, s1, g, w_in, conv_w, conv_b, w_out)


def _mlp_ple_kernel(*refs, has_attn, has_final, ff, ff_chunk):
    refs = list(refs)
    h_ref = refs.pop(0)
    if has_attn:
        o_ref = refs.pop(0)
        wo_ref = refs.pop(0)
    mg_ref, wup_ref, wdown_ref, pg_ref, wgate_ref, p_ref, wproj_ref = refs[:7]
    refs = refs[7:]
    if has_final:
        fg_ref = refs.pop(0)
    out_ref = refs.pop(0)

    h = h_ref[...]
    if has_attn:
        h = h + _dot(o_ref[...], wo_ref[...])
    m = (_rms_hat(h) * mg_ref[...]).astype(BF16)
    acc = h
    for c in range(ff // ff_chunk):
        lo = c * ff_chunk
        a = _dot(m, wup_ref[:, lo:lo + ff_chunk])
        a = jnp.square(jnp.maximum(a, 0.0)).astype(BF16)
        acc = acc + _dot(a, wdown_ref[lo:lo + ff_chunk, :])
    gate = jax.nn.sigmoid(_dot((_rms_hat(acc) * pg_ref[...]).astype(BF16), wgate_ref[...]))
    h3 = acc + gate * _dot(p_ref[...].astype(BF16), wproj_ref[...])
    if has_final:
        h3 = _rms_hat(h3) * fg_ref[...]
    out_ref[...] = h3


def _mlp_ple(h, p_layers, layer, mg, w_up, w_down, pg, w_gate, w_proj, attn=None, final_g=None):
    n, d = h.shape
    ff = w_up.shape[1]
    ple = p_layers.shape[2]
    tm = min(TOKEN_TILE, n)
    assert n % tm == 0
    ff_chunk = min(FF_CHUNK, ff)
    assert ff % ff_chunk == 0
    row = lambda i: (i, 0)
    args = [h]
    specs = [pl.BlockSpec((tm, d), row)]
    if attn is not None:
        o, wo = attn
        args += [o, wo]
        specs += [pl.BlockSpec((tm, o.shape[1]), row), _resident(wo.shape)]
    args += [mg, w_up, w_down, pg, w_gate, p_layers, w_proj]
    specs += [_resident((1, d)), _resident((d, ff)), _resident((ff, d)), _resident((1, d)),
              _resident((d, d)), pl.BlockSpec((None, tm, ple), lambda i: (layer, i, 0)),
              _resident((ple, d))]
    if final_g is not None:
        args.append(final_g)
        specs.append(_resident((1, d)))
    kern = functools.partial(_mlp_ple_kernel, has_attn=attn is not None,
                             has_final=final_g is not None, ff=ff, ff_chunk=ff_chunk)
    return pl.pallas_call(
        kern,
        out_shape=jax.ShapeDtypeStruct((n, d), F32),
        grid=(n // tm,),
        in_specs=specs,
        out_specs=pl.BlockSpec((tm, d), row),
        compiler_params=_params("arbitrary"),
        name="mlp_ple",
    )(*args)


def _rope_tables(pos, head_dim):
    rope_dim = head_dim // 4
    half = rope_dim // 2
    inv = jnp.power(jnp.float32(ROPE_THETA),
                    -jnp.arange(0, rope_dim, 2, dtype=jnp.float32) / rope_dim)
    ang = pos.astype(jnp.float32)[:, None] * inv[None, :]
    cos = jnp.cos(ang)
    sin = jnp.sin(ang)
    dcol = jnp.arange(2 * head_dim) % head_dim
    j = dcol % half
    c = jnp.where(dcol[None, :] < rope_dim, cos[:, j], 1.0)
    sa = jnp.where(dcol[None, :] < half, -sin[:, j], 0.0)
    sb = jnp.where((dcol[None, :] >= half) & (dcol[None, :] < rope_dim), sin[:, j], 0.0)
    return c.astype(F32), sa.astype(F32), sb.astype(F32)


def _rope_head(t, c, sa, sb, half):
    width = t.shape[-1]
    return t * c + pltpu.roll(t, width - half, 1) * sa + pltpu.roll(t, half, 1) * sb


def _qkv_kernel(h_ref, kvg_ref, qg_ref, wk_ref, wv_ref, wq_ref, c_ref, sa_ref, sb_ref,
                k_ref, v_ref, qb_ref, *attn_refs, n_heads, half, q_scale):
    tm = h_ref.shape[1]
    h = h_ref[0]
    xh = _rms_hat(h)
    hk = (xh * kvg_ref[...]).astype(BF16)
    hq = (xh * qg_ref[...]).astype(BF16)
    c = c_ref[...]
    sa = sa_ref[...]
    sb = sb_ref[...]
    k = _dot(hk, wk_ref[...])
    q = _dot(hq, wq_ref[...])
    v = _dot(hk, wv_ref[...])
    v_ref[0] = v
    for hd in range(n_heads):
        cols = slice(hd * LANES, (hd + 1) * LANES)
        kh = _rope_head(k[:, cols], c, sa, sb, half)
        k_ref[0, :, cols] = kh
        qb_ref[0, :, cols] = (_rope_head(q[:, cols], c, sa, sb, half) * q_scale).astype(BF16)
        if attn_refs:
            attn_refs[0][0, :, cols] = kh.astype(BF16)
    if attn_refs:
        vt = v.T.astype(BF16)
        va = LANES + VT_ONES_ROWS
        for hd in range(n_heads):
            attn_refs[1][0, 0, hd * va:hd * va + LANES, :] = vt[hd * LANES:(hd + 1) * LANES, :]
            attn_refs[1][0, 0, hd * va + LANES:(hd + 1) * va, :] = jnp.ones((VT_ONES_ROWS, tm), BF16)


def _qkv(h, kvg, qg, w_k, w_v, w_q, tables, n_heads, head_dim, for_attention):
    b, s, d = h.shape
    tm = min(TOKEN_TILE, s)
    assert s % tm == 0 and 2 * head_dim == LANES
    nt = s // tm
    tok = lambda i, t: (i, t, 0)
    tab = pl.BlockSpec((tm, LANES), lambda i, t: (t, 0))
    out_shape = [jax.ShapeDtypeStruct((b, s, d), F32), jax.ShapeDtypeStruct((b, s, d), F32),
                 jax.ShapeDtypeStruct((b, s, d), BF16)]
    out_specs = [pl.BlockSpec((1, tm, d), tok)] * 3
    if for_attention:
        vt_rows = n_heads * (LANES + VT_ONES_ROWS)
        out_shape += [jax.ShapeDtypeStruct((b, s, d), BF16),
                      jax.ShapeDtypeStruct((b, nt, vt_rows, tm), BF16)]
        out_specs += [pl.BlockSpec((1, tm, d), tok),
                      pl.BlockSpec((1, 1, vt_rows, tm), lambda i, t: (i, t, 0, 0))]
    kern = functools.partial(_qkv_kernel, n_heads=n_heads, half=head_dim // 8,
                             q_scale=head_dim ** -0.5 * LOG2E)
    return pl.pallas_call(
        kern,
        out_shape=tuple(out_shape),
        grid=(b, nt),
        in_specs=[pl.BlockSpec((1, tm, d), tok), _resident((1, d)), _resident((1, d)),
                  _resident((d, d)), _resident((d, d)), _resident((d, d)), tab, tab, tab],
        out_specs=tuple(out_specs),
        compiler_params=_params("arbitrary", "arbitrary"),
        name="qkv_rope",
    )(h, kvg, qg, w_k, w_v, w_q, *tables)


def _lambda_value(lq1_ref, lk1_ref, lq2_ref, lk2_ref, lam_init):
    s1 = jnp.sum(lq1_ref[...] * lk1_ref[...], axis=-1, keepdims=True)
    s2 = jnp.sum(lq2_ref[...] * lk2_ref[...], axis=-1, keepdims=True)
    return jnp.exp(s1) - jnp.exp(s2) + lam_init


def _prompt_block(lam, sg_ref, q_ref, k_ref, vt_ref, o_ref, m_ref, acc_ref, sa_ref, sb_ref,
                  *, qi, blk, head_dim, vdim, lam_init, alongside_first_scores):
    q = q_ref[0]
    lane = lax.broadcasted_iota(jnp.int32, q.shape, 1)
    zero = jnp.zeros_like(q)
    q_comp = (jnp.where(lane < head_dim, q, zero), jnp.where(lane >= head_dim, q, zero))

    m_ref[...] = jnp.full(m_ref.shape, NEG, F32)
    acc_ref[...] = jnp.zeros(acc_ref.shape, F32)

    def scores(j, s_ref):
        start = pl.multiple_of(j * blk, blk)
        kb = k_ref[0, pl.ds(start, blk), :]
        for c in range(2):
            s_ref[c] = _dot_nt(kb, q_comp[c])

    def update(j, s_ref, masked):
        vt = vt_ref[0, j]
        for c in range(2):
            s = s_ref[c]
            if masked:
                kpos = lax.broadcasted_iota(jnp.int32, s.shape, 0)
                qpos = lax.broadcasted_iota(jnp.int32, s.shape, 1)
                s = jnp.where(kpos <= qpos, s, NEG)
            m_old = m_ref[c]
            m_new = jnp.maximum(m_old, jnp.max(s, axis=0, keepdims=True))
            alpha = jnp.exp2(m_old - m_new)
            p = jnp.exp2(s - m_new).astype(BF16)
            acc_ref[c] = alpha * acc_ref[c] + _dot(vt, p)
            m_ref[c] = m_new

    scores(0, sa_ref)
    alongside_first_scores()

    def body(i, carry):
        j = 2 * i
        scores(j + 1, sb_ref)
        update(j, sa_ref, False)
        scores(j + 2, sa_ref)
        update(j + 1, sb_ref, False)
        return carry

    lax.fori_loop(0, qi // 2, body, 0)

    @pl.when(qi % 2 == 1)
    def _():
        scores(qi, sb_ref)
        update(qi - 1, sa_ref, False)
        update(qi, sb_ref, True)

    @pl.when(qi % 2 == 0)
    def _():
        update(qi, sa_ref, True)

    inv1 = 1.0 / acc_ref[0, vdim:vdim + 1, :]
    inv2 = lam / acc_ref[1, vdim:vdim + 1, :]
    out = acc_ref[0, 0:vdim, :] * inv1 - acc_ref[1, 0:vdim, :] * inv2
    ms = jnp.mean(out * out, axis=0, keepdims=True)
    y = out * lax.rsqrt(ms + EPS) * sg_ref[...] * (1.0 - lam_init)
    o_ref[0] = y.T.astype(BF16)


def _decode_query_rows(q_ref, *, n_heads, head_dim):
    rows = 2 * n_heads
    d = q_ref.shape[-1]
    qrow = q_ref[0].astype(F32)
    row = lax.broadcasted_iota(jnp.int32, (rows, d), 0)
    col = lax.broadcasted_iota(jnp.int32, (rows, d), 1)
    own = (col // head_dim) == 2 * (row % n_heads) + row // n_heads
    return jnp.where(own, jnp.broadcast_to(qrow, (rows, d)), 0.0)


def _decode_start(q_ref, kn_ref, vn_ref, m_ref, l_ref, acc_ref, *, n_heads, head_dim):
    qm = _decode_query_rows(q_ref, n_heads=n_heads, head_dim=head_dim)
    m_ref[...] = jnp.sum(qm * kn_ref[0], axis=1, keepdims=True)
    l_ref[...] = jnp.ones(l_ref.shape, F32)
    acc_ref[...] = jnp.concatenate([vn_ref[0], vn_ref[0]], axis=0)


def _decode_finish(lam, sg_ref, o_ref, l_ref, acc_ref, *, n_heads, lam_init):
    w = acc_ref[...] / l_ref[...]
    out = w[0:n_heads] - lam * w[n_heads:2 * n_heads]
    o_ref[0] = (_rms_hat(out) * sg_ref[...] * (1.0 - lam_init)).astype(BF16)


def _decode_pages(q_ref, e_ref, k_refs, v_refs, m_ref, l_ref, acc_ref, *, n_heads, head_dim):
    pages = len(k_refs)
    rows = 2 * n_heads
    page = k_refs[0].shape[-1]
    vdim = v_refs[0].shape[-1]
    qm = _decode_query_rows(q_ref, n_heads=n_heads, head_dim=head_dim).astype(BF16)
    srow = lax.broadcasted_iota(jnp.int32, (rows, page * n_heads), 0)
    scol = lax.broadcasted_iota(jnp.int32, (rows, page * n_heads), 1)
    own_head = scol % n_heads == srow % n_heads

    pair = 2 if pages % 2 == 0 else 1

    def side_by_side(refs, g):
        return jnp.concatenate([refs[g + u][0].astype(BF16) for u in range(pair)], axis=1)

    s = jnp.concatenate([_dot(qm, side_by_side(k_refs, g)) for g in range(0, pages, pair)],
                        axis=1)
    m_old = m_ref[...]
    m_new = jnp.maximum(m_old, jnp.max(s, axis=1, keepdims=True))
    alpha = jnp.exp2(m_old - m_new)
    p = jnp.exp2(s - m_new).astype(BF16)
    l_ref[...] = alpha * l_ref[...] + jnp.sum(p.astype(F32), axis=1, keepdims=True)
    m_ref[...] = m_new
    p_stack = jnp.concatenate([p[:, g * page:(g + 1) * page] for g in range(pages)], axis=0)
    spread = _dot(p_stack, e_ref[...])
    pv = None
    for g in range(0, pages, pair):
        own = jnp.concatenate([own_head] * pair, axis=0)
        p_exp = jnp.where(own, spread[g * rows:(g + pair) * rows], 0.0).astype(BF16)
        r = _dot(p_exp, side_by_side(v_refs, g))
        for u in range(pair):
            term = r[u * rows:(u + 1) * rows, u * vdim:(u + 1) * vdim]
            pv = term if pv is None else pv + term
    acc_ref[...] = alpha * acc_ref[...] + pv


def _attention_kernel(pt_ref, lq1_ref, lk1_ref, lq2_ref, lk2_ref, sgc_ref, sgr_ref,
                      q_ref, k_ref, vt_ref, dq_ref, kn_ref, vn_ref, e_ref, *rest,
                      pages, n_dsteps, blk, n_heads, head_dim, vdim, lam_init):
    del pt_ref
    k_refs = rest[:pages]
    v_refs = rest[pages:2 * pages]
    o_ref, od_ref, m_ref, acc_ref, sa_ref, sb_ref, dm_ref, dl_ref, dacc_ref = rest[2 * pages:]
    lam = _lambda_value(lq1_ref, lk1_ref, lq2_ref, lk2_ref, lam_init)
    step = (pl.program_id(0) * pl.num_programs(1) + pl.program_id(1)) * pl.num_programs(2) \
        + pl.program_id(2)
    dstep = step % n_dsteps

    @pl.when(dstep == 0)
    def _():
        _decode_start(dq_ref, kn_ref, vn_ref, dm_ref, dl_ref, dacc_ref,
                      n_heads=n_heads, head_dim=head_dim)

    decode = functools.partial(_decode_pages, dq_ref, e_ref, k_refs, v_refs, dm_ref, dl_ref,
                               dacc_ref, n_heads=n_heads, head_dim=head_dim)
    _prompt_block(lam, sgc_ref, q_ref, k_ref, vt_ref, o_ref, m_ref, acc_ref, sa_ref, sb_ref,
                  qi=pl.program_id(2), blk=blk, head_dim=head_dim, vdim=vdim, lam_init=lam_init,
                  alongside_first_scores=decode)

    @pl.when(dstep == n_dsteps - 1)
    def _():
        _decode_finish(lam, sgr_ref, od_ref, dl_ref, dacc_ref, n_heads=n_heads, lam_init=lam_init)


def _attention(q, k, vt, q_s, k_new, v_new, cache_k, cache_v, page_table, lam_vecs, sub_g,
               n_heads, head_dim, lam_init):
    b, s, d = q.shape
    blk = vt.shape[-1]
    nq = s // blk
    vdim = d // n_heads
    va = vdim + VT_ONES_ROWS
    assert vdim == LANES and 2 * head_dim == LANES and vt.shape[2] == n_heads * va
    n = q_s.shape[0]
    n_pool, page = cache_k.shape[:2]
    n_pages = page_table.shape[1]
    steps = b * n_heads * nq
    assert steps % n == 0 and n_pages % (steps // n) == 0
    n_dsteps = steps // n
    pages = n_pages // n_dsteps
    rows = 2 * n_heads
    kt = jnp.transpose(cache_k, (0, 2, 3, 4, 1)).reshape(n_pool, d, page)
    v2 = cache_v.reshape(n_pool, page * n_heads, vdim)
    spread = (jnp.arange(page * n_heads)[None, :] // n_heads
              == jnp.arange(page)[:, None]).astype(BF16)

    def lin(i, h, t):
        return (i * n_heads + h) * nq + t

    def seq_of(i, h, t):
        return lin(i, h, t) // n_dsteps

    def page_of(g):
        return lambda i, h, t, pt: (pt[lin(i, h, t), g], 0, 0)

    const = lambda shape: pl.BlockSpec(shape, lambda i, h, t, pt: (0,) * len(shape))
    per_seq = lambda shape: pl.BlockSpec((1,) + shape, lambda i, h, t, pt: (seq_of(i, h, t), 0, 0))
    kern = functools.partial(_attention_kernel, pages=pages, n_dsteps=n_dsteps, blk=blk,
                             n_heads=n_heads, head_dim=head_dim, vdim=vdim, lam_init=lam_init)
    grid_spec = pltpu.PrefetchScalarGridSpec(
        num_scalar_prefetch=1,
        grid=(b, n_heads, nq),
        in_specs=[const((1, head_dim))] * 4 + [const((vdim, 1)), const((1, vdim)),
                  pl.BlockSpec((1, blk, LANES), lambda i, h, t, pt: (i, t, h)),
                  pl.BlockSpec((1, s, LANES), lambda i, h, t, pt: (i, 0, h)),
                  pl.BlockSpec((1, nq, va, blk), lambda i, h, t, pt: (i, 0, h, 0)),
                  per_seq((1, d)), per_seq((1, d)), per_seq((n_heads, vdim)),
                  const((page, page * n_heads))]
                 + [pl.BlockSpec((1, d, page), page_of(g)) for g in range(pages)]
                 + [pl.BlockSpec((1, page * n_heads, vdim), page_of(g)) for g in range(pages)],
        out_specs=(pl.BlockSpec((1, blk, LANES), lambda i, h, t, pt: (i, t, h)),
                   per_seq((n_heads, vdim))),
        scratch_shapes=[pltpu.VMEM((2, 1, blk), F32), pltpu.VMEM((2, va, blk), F32),
                        pltpu.VMEM((2, blk, blk), F32), pltpu.VMEM((2, blk, blk), F32),
                        pltpu.VMEM((rows, 1), F32), pltpu.VMEM((rows, 1), F32),
                        pltpu.VMEM((rows, vdim), F32)],
    )
    o, o_s = pl.pallas_call(
        kern,
        out_shape=(jax.ShapeDtypeStruct((b, s, d), BF16),
                   jax.ShapeDtypeStruct((n, n_heads, vdim), BF16)),
        grid_spec=grid_spec,
        compiler_params=_params("arbitrary", "arbitrary", "arbitrary"),
        name="diff_attention",
    )(page_table.reshape(steps, pages), *lam_vecs, sub_g.reshape(-1, 1), sub_g.reshape(1, -1), q, k, vt,
      q_s.reshape(n, 1, d), k_new.reshape(n, 1, d), v_new.reshape(n, n_heads, vdim),
      spread, *([kt] * pages), *([v2] * pages))
    return o, o_s.reshape(n, d)


def kernel(x_prompt, x_sample, p_prompt, p_sample, state_conv, cache_k, cache_v, page_table,
           a_norm, a_w_in, a_conv_w, a_conv_b, a_w_out, kv_norm, w_k, w_v, b_norm, b_wq,
           b_lq1, b_lk1, b_lq2, b_lk2, b_subln, b_wo, mlp_norm, w_up, w_down, ple_norm,
           w_ple_gate, w_ple_proj, final_norm):
    batch, seq, d = x_prompt.shape
    dec_batch, dec_seq, _ = x_sample.shape
    depth = p_prompt.shape[0]
    n_pool, page, n_heads, two, head_dim = cache_k.shape
    n_a = state_conv.shape[0]
    assert depth == 2 and n_a == 1 and two == 2 and dec_seq == 1
    past_len = page_table.shape[1] * page
    li_b = n_a
    lam_init = _lambda_init(li_b)

    bf = lambda w: w.astype(BF16)
    row = lambda g: g.reshape(1, -1).astype(F32)
    w_in, w_out = bf(a_w_in[0]), bf(a_w_out[0])
    wk, wv, wq, wo = bf(w_k), bf(w_v), bf(b_wq[0]), bf(b_wo[0])
    lam_vecs = tuple(row(v[0]) for v in (b_lq1, b_lk1, b_lq2, b_lk2))
    mlp_weights = [(row(mlp_norm[li]), bf(w_up[li]), bf(w_down[li]), row(ple_norm[li]),
                    bf(w_ple_gate[li]), bf(w_ple_proj[li])) for li in range(depth)]
    pp = p_prompt.reshape(depth, batch * seq, -1)
    ps = p_sample.reshape(depth, dec_batch, -1)

    zeros_state = jnp.zeros((batch, CONV_W - 1, d), F32)
    h, conv_p = _mixer_seq(x_prompt, zeros_state, row(a_norm[0]), w_in, a_conv_w[0],
                           row(a_conv_b[0]), w_out)
    h = _mlp_ple(h.reshape(batch * seq, d), pp, 0, *mlp_weights[0])
    tables_p = _rope_tables(jnp.arange(seq), head_dim)
    k_p, v_p, q_b, k_b, vt_b = _qkv(h.reshape(batch, seq, d), row(kv_norm), row(b_norm[0]),
                                    wk, wv, wq, tables_p, n_heads, head_dim, True)

    xs = x_sample.reshape(dec_batch, d)
    hs, u_s = _mixer_step(xs, state_conv[0, :, 0], state_conv[0, :, 1], row(a_norm[0]), w_in,
                          a_conv_w[0], row(a_conv_b[0]), w_out)
    conv_s = jnp.stack([state_conv[0, :, 1], u_s], axis=1)
    hs = _mlp_ple(hs, ps, 0, *mlp_weights[0])
    tables_s = _rope_tables(jnp.full((dec_batch,), past_len, jnp.int32), head_dim)
    k_s, v_s, q_s = _qkv(hs.reshape(1, dec_batch, d), row(kv_norm), row(b_norm[0]),
                         wk, wv, wq, tables_s, n_heads, head_dim, False)
    k_s, v_s, q_s = (a.reshape(dec_batch, d) for a in (k_s, v_s, q_s))

    o, o_s = _attention(q_b, k_b, vt_b, q_s, k_s, v_s, cache_k, cache_v, page_table, lam_vecs,
                        b_subln[0].astype(F32), n_heads, head_dim, lam_init)
    y_prompt = _mlp_ple(h, pp, 1, *mlp_weights[1],
                        attn=(o.reshape(batch * seq, d), wo), final_g=row(final_norm))
    y_sample = _mlp_ple(hs, ps, 1, *mlp_weights[1],
                        attn=(o_s, wo), final_g=row(final_norm))

    return (y_prompt.reshape(batch, seq, d),
            y_sample.reshape(dec_batch, dec_seq, d),
            conv_p[None],
            conv_s[None],
            k_p.reshape(batch, seq, n_heads, 2, head_dim),
            v_p.reshape(batch, seq, n_heads, d // n_heads),
            k_s.reshape(dec_batch, dec_seq, n_heads, 2, head_dim),
            v_s.reshape(dec_batch, dec_seq, n_heads, d // n_heads))
```

```python
import functools
import math

import jax
import jax.numpy as jnp
from jax import lax
from jax.experimental import pallas as pl
from jax.experimental.pallas import tpu as pltpu

F32 = jnp.float32
BF16 = jnp.bfloat16

EPS = 1e-6
NEG = -1e30
ROPE_THETA = 500000.0
CONV_W = 3

V7X_VMEM_LIMIT_BYTES = 56 * 1024 * 1024
LANES = 128

TOKEN_TILE = 512
FF_CHUNK = 1024
VT_ONES_ROWS = 16
LOG2E = math.log2(math.e)
KEY_LOOP_UNROLL = 4


def _lambda_init(layer_idx):
    return 0.8 - 0.6 * math.exp(-0.3 * layer_idx)


def _rms_hat(x):
    return x * lax.rsqrt(jnp.mean(x * x, axis=-1, keepdims=True) + EPS)


def _dot(a, b):
    return jnp.dot(a, b, preferred_element_type=F32)


def _dot_nt(a, b):
    return lax.dot_general(a, b, (((1,), (1,)), ((), ())), preferred_element_type=F32)


def _resident(shape):
    nd = len(shape)
    return pl.BlockSpec(shape, lambda *_: (0,) * nd, pipeline_mode=pl.Buffered(1))


def _params(*sem):
    return pltpu.CompilerParams(dimension_semantics=sem, vmem_limit_bytes=V7X_VMEM_LIMIT_BYTES)


def _mixer_seq_kernel(x_ref, st_ref, g_ref, win_ref, cw_ref, cb_ref, wout_ref,
                      h_ref, stout_ref, ubuf, *, tm, d):
    @pl.when(pl.program_id(1) == 0)
    def _():
        ubuf[6:8, :] = st_ref[0]

    x = x_ref[0]
    hn = (_rms_hat(x) * g_ref[...]).astype(BF16)
    gc = _dot(hn, win_ref[:, d:2 * d])
    xin = _dot(hn, win_ref[:, 2 * d:3 * d])
    u = gc * xin
    ubuf[8:8 + tm, :] = u
    conv = (cb_ref[...] + ubuf[6:6 + tm, :] * cw_ref[0:1, :]
            + ubuf[7:7 + tm, :] * cw_ref[1:2, :] + u * cw_ref[2:3, :])
    gb = _dot(hn, win_ref[:, 0:d])
    y = _dot((gb * conv).astype(BF16), wout_ref[...])
    h_ref[0] = x + y
    last = ubuf[6 + tm:8 + tm, :]
    stout_ref[0] = last
    ubuf[6:8, :] = last


def _mixer_seq(x, state, g, w_in, conv_w, conv_b, w_out):
    b, s, d = x.shape
    tm = min(TOKEN_TILE, s)
    assert s % tm == 0 and tm >= CONV_W - 1
    kern = functools.partial(_mixer_seq_kernel, tm=tm, d=d)
    return pl.pallas_call(
        kern,
        out_shape=(jax.ShapeDtypeStruct((b, s, d), F32),
                   jax.ShapeDtypeStruct((b, CONV_W - 1, d), F32)),
        grid=(b, s // tm),
        in_specs=[
            pl.BlockSpec((1, tm, d), lambda i, t: (i, t, 0)),
            pl.BlockSpec((1, CONV_W - 1, d), lambda i, t: (i, 0, 0)),
            _resident((1, d)),
            _resident((d, 3 * d)),
            _resident((CONV_W, d)),
            _resident((1, d)),
            _resident((d, d)),
        ],
        out_specs=(pl.BlockSpec((1, tm, d), lambda i, t: (i, t, 0)),
                   pl.BlockSpec((1, CONV_W - 1, d), lambda i, t: (i, 0, 0))),
        scratch_shapes=[pltpu.VMEM((tm + 8, d), F32)],
        compiler_params=_params("arbitrary", "arbitrary"),
        name="mixer_seq",
    )(x, state, g, w_in, conv_w, conv_b, w_out)


def _mixer_step_kernel(x_ref, s0_ref, s1_ref, g_ref, win_ref, cw_ref, cb_ref, wout_ref,
                       h_ref, u_ref, *, d):
    x = x_ref[...]
    hn = (_rms_hat(x) * g_ref[...]).astype(BF16)
    gc = _dot(hn, win_ref[:, d:2 * d])
    xin = _dot(hn, win_ref[:, 2 * d:3 * d])
    u = gc * xin
    conv = (cb_ref[...] + s0_ref[...] * cw_ref[0:1, :]
            + s1_ref[...] * cw_ref[1:2, :] + u * cw_ref[2:3, :])
    gb = _dot(hn, win_ref[:, 0:d])
    y = _dot((gb * conv).astype(BF16), wout_ref[...])
    h_ref[...] = x + y
    u_ref[...] = u


def _mixer_step(x, s0, s1, g, w_in, conv_w, conv_b, w_out):
    n, d = x.shape
    kern = functools.partial(_mixer_step_kernel, d=d)
    return pl.pallas_call(
        kern,
        out_shape=(jax.ShapeDtypeStruct((n, d), F32), jax.ShapeDtypeStruct((n, d), F32)),
        compiler_params=pltpu.CompilerParams(vmem_limit_bytes=V7X_VMEM_LIMIT_BYTES),
        name="mixer_step",
    )(x, s0, s1, g, w_in, conv_w, conv_b, w_out)


def _mlp_ple_kernel(*refs, has_attn, has_final, ff, ff_chunk):
    refs = list(refs)
    h_ref = refs.pop(0)
    if has_attn:
        o_ref = refs.pop(0)
        wo_ref = refs.pop(0)
    mg_ref, wup_ref, wdown_ref, pg_ref, wgate_ref, p_ref, wproj_ref = refs[:7]
    refs = refs[7:]
    if has_final:
        fg_ref = refs.pop(0)
    out_ref = refs.pop(0)

    h = h_ref[...]
    if has_attn:
        h = h + _dot(o_ref[...], wo_ref[...])
    m = (_rms_hat(h) * mg_ref[...]).astype(BF16)
    acc = h
    for c in range(ff // ff_chunk):
        lo = c * ff_chunk
        a = _dot(m, wup_ref[:, lo:lo + ff_chunk])
        a = jnp.square(jnp.maximum(a, 0.0)).astype(BF16)
        acc = acc + _dot(a, wdown_ref[lo:lo + ff_chunk, :])
    gate = jax.nn.sigmoid(_dot((_rms_hat(acc) * pg_ref[...]).astype(BF16), wgate_ref[...]))
    h3 = acc + gate * _dot(p_ref[...].astype(BF16), wproj_ref[...])
    if has_final:
        h3 = _rms_hat(h3) * fg_ref[...]
    out_ref[...] = h3


def _mlp_ple(h, p_layers, layer, mg, w_up, w_down, pg, w_gate, w_proj, attn=None, final_g=None):
    n, d = h.shape
    ff = w_up.shape[1]
    ple = p_layers.shape[2]
    tm = min(TOKEN_TILE, n)
    assert n % tm == 0
    ff_chunk = min(FF_CHUNK, ff)
    assert ff % ff_chunk == 0
    row = lambda i: (i, 0)
    args = [h]
    specs = [pl.BlockSpec((tm, d), row)]
    if attn is not None:
        o, wo = attn
        args += [o, wo]
        specs += [pl.BlockSpec((tm, o.shape[1]), row), _resident(wo.shape)]
    args += [mg, w_up, w_down, pg, w_gate, p_layers, w_proj]
    specs += [_resident((1, d)), _resident((d, ff)), _resident((ff, d)), _resident((1, d)),
              _resident((d, d)), pl.BlockSpec((None, tm, ple), lambda i: (layer, i, 0)),
              _resident((ple, d))]
    if final_g is not None:
        args.append(final_g)
        specs.append(_resident((1, d)))
    kern = functools.partial(_mlp_ple_kernel, has_attn=attn is not None,
                             has_final=final_g is not None, ff=ff, ff_chunk=ff_chunk)
    return pl.pallas_call(
        kern,
        out_shape=jax.ShapeDtypeStruct((n, d), F32),
        grid=(n // tm,),
        in_specs=specs,
        out_specs=pl.BlockSpec((tm, d), row),
        compiler_params=_params("arbitrary"),
        name="mlp_ple",
    )(*args)


def _rope_tables(pos, head_dim):
    rope_dim = head_dim // 4
    half = rope_dim // 2
    inv = jnp.power(jnp.float32(ROPE_THETA),
                    -jnp.arange(0, rope_dim, 2, dtype=jnp.float32) / rope_dim)
    ang = pos.astype(jnp.float32)[:, None] * inv[None, :]
    cos = jnp.cos(ang)
    sin = jnp.sin(ang)
    dcol = jnp.arange(2 * head_dim) % head_dim
    j = dcol % half
    c = jnp.where(dcol[None, :] < rope_dim, cos[:, j], 1.0)
    sa = jnp.where(dcol[None, :] < half, -sin[:, j], 0.0)
    sb = jnp.where((dcol[None, :] >= half) & (dcol[None, :] < rope_dim), sin[:, j], 0.0)
    return c.astype(F32), sa.astype(F32), sb.astype(F32)


def _rope_head(t, c, sa, sb, half):
    width = t.shape[-1]
    return t * c + pltpu.roll(t, width - half, 1) * sa + pltpu.roll(t, half, 1) * sb


def _qkv_kernel(h_ref, kvg_ref, qg_ref, wk_ref, wv_ref, wq_ref, c_ref, sa_ref, sb_ref,
                k_ref, v_ref, qb_ref, *attn_refs, n_heads, half, q_scale):
    tm = h_ref.shape[1]
    h = h_ref[0]
    xh = _rms_hat(h)
    hk = (xh * kvg_ref[...]).astype(BF16)
    hq = (xh * qg_ref[...]).astype(BF16)
    c = c_ref[...]
    sa = sa_ref[...]
    sb = sb_ref[...]
    k = _dot(hk, wk_ref[...])
    q = _dot(hq, wq_ref[...])
    v = _dot(hk, wv_ref[...])
    v_ref[0] = v
    for hd in range(n_heads):
        cols = slice(hd * LANES, (hd + 1) * LANES)
        kh = _rope_head(k[:, cols], c, sa, sb, half)
        k_ref[0, :, cols] = kh
        qb_ref[0, :, cols] = (_rope_head(q[:, cols], c, sa, sb, half) * q_scale).astype(BF16)
        if attn_refs:
            attn_refs[0][0, :, cols] = kh.astype(BF16)
    if attn_refs:
        vt = v.T.astype(BF16)
        va = LANES + VT_ONES_ROWS
        for hd in range(n_heads):
            attn_refs[1][0, 0, hd * va:hd * va + LANES, :] = vt[hd * LANES:(hd + 1) * LANES, :]
            attn_refs[1][0, 0, hd * va + LANES:(hd + 1) * va, :] = jnp.ones((VT_ONES_ROWS, tm), BF16)


def _qkv(h, kvg, qg, w_k, w_v, w_q, tables, n_heads, head_dim, for_attention):
    b, s, d = h.shape
    tm = min(TOKEN_TILE, s)
    assert s % tm == 0 and 2 * head_dim == LANES
    nt = s // tm
    tok = lambda i, t: (i, t, 0)
    tab = pl.BlockSpec((tm, LANES), lambda i, t: (t, 0))
    out_shape = [jax.ShapeDtypeStruct((b, s, d), F32), jax.ShapeDtypeStruct((b, s, d), F32),
                 jax.ShapeDtypeStruct((b, s, d), BF16)]
    out_specs = [pl.BlockSpec((1, tm, d), tok)] * 3
    if for_attention:
        vt_rows = n_heads * (LANES + VT_ONES_ROWS)
        out_shape += [jax.ShapeDtypeStruct((b, s, d), BF16),
                      jax.ShapeDtypeStruct((b, nt, vt_rows, tm), BF16)]
        out_specs += [pl.BlockSpec((1, tm, d), tok),
                      pl.BlockSpec((1, 1, vt_rows, tm), lambda i, t: (i, t, 0, 0))]
    kern = functools.partial(_qkv_kernel, n_heads=n_heads, half=head_dim // 8,
                             q_scale=head_dim ** -0.5 * LOG2E)
    return pl.pallas_call(
        kern,
        out_shape=tuple(out_shape),
        grid=(b, nt),
        in_specs=[pl.BlockSpec((1, tm, d), tok), _resident((1, d)), _resident((1, d)),
                  _resident((d, d)), _resident((d, d)), _resident((d, d)), tab, tab, tab],
        out_specs=tuple(out_specs),
        compiler_params=_params("arbitrary", "arbitrary"),
        name="qkv_rope",
    )(h, kvg, qg, w_k, w_v, w_q, *tables)


def _lambda_value(lq1_ref, lk1_ref, lq2_ref, lk2_ref, lam_init):
    s1 = jnp.sum(lq1_ref[...] * lk1_ref[...], axis=-1, keepdims=True)
    s2 = jnp.sum(lq2_ref[...] * lk2_ref[...], axis=-1, keepdims=True)
    return jnp.exp(s1) - jnp.exp(s2) + lam_init


def _prompt_block(lam, sg_ref, q_ref, k_ref, vt_ref, o_ref, m_ref, acc_ref, sa_ref, sb_ref,
                  *, qi, blk, head_dim, vdim, lam_init, alongside_first_scores):
    q = q_ref[0]
    lane = lax.broadcasted_iota(jnp.int32, q.shape, 1)
    zero = jnp.zeros_like(q)
    q_comp = (jnp.where(lane < head_dim, q, zero), jnp.where(lane >= head_dim, q, zero))

    m_ref[...] = jnp.full(m_ref.shape, NEG, F32)
    acc_ref[...] = jnp.zeros(acc_ref.shape, F32)

    def scores(j, s_ref):
        start = pl.multiple_of(j * blk, blk)
        kb = k_ref[0, pl.ds(start, blk), :]
        for c in range(2):
            s_ref[c] = _dot_nt(kb, q_comp[c])

    def update(j, s_ref, masked):
        vt = vt_ref[0, j]
        for c in range(2):
            s = s_ref[c]
            if masked:
                kpos = lax.broadcasted_iota(jnp.int32, s.shape, 0)
                qpos = lax.broadcasted_iota(jnp.int32, s.shape, 1)
                s = jnp.where(kpos <= qpos, s, NEG)
            m_old = m_ref[c]
            m_new = jnp.maximum(m_old, jnp.max(s, axis=0, keepdims=True))
            alpha = jnp.exp2(m_old - m_new)
            p = jnp.exp2(s - m_new).astype(BF16)
            acc_ref[c] = alpha * acc_ref[c] + _dot(vt, p)
            m_ref[c] = m_new

    scores(0, sa_ref)
    alongside_first_scores()

    bufs = (sa_ref, sb_ref)

    def run(j0, n_unmasked, masked_last):
        total = n_unmasked + (1 if masked_last else 0)
        for u in range(total):
            if u + 1 < total or not masked_last:
                scores(j0 + u + 1, bufs[(u + 1) % 2])
            update(j0 + u, bufs[u % 2], masked_last and u == total - 1)

    def body(i, carry):
        run(KEY_LOOP_UNROLL * i, KEY_LOOP_UNROLL, False)
        return carry

    lax.fori_loop(0, qi // KEY_LOOP_UNROLL, body, 0)
    j_rem = (qi // KEY_LOOP_UNROLL) * KEY_LOOP_UNROLL
    for r in range(KEY_LOOP_UNROLL):
        @pl.when(qi % KEY_LOOP_UNROLL == r)
        def _(r=r):
            run(j_rem, r, True)

    inv1 = 1.0 / acc_ref[0, vdim:vdim + 1, :]
    inv2 = lam / acc_ref[1, vdim:vdim + 1, :]
    out = acc_ref[0, 0:vdim, :] * inv1 - acc_ref[1, 0:vdim, :] * inv2
    ms = jnp.mean(out * out, axis=0, keepdims=True)
    y = out * lax.rsqrt(ms + EPS) * sg_ref[...] * (1.0 - lam_init)
    o_ref[0] = y.T.astype(BF16)


def _decode_query_rows(q_ref, *, n_heads, head_dim):
    rows = 2 * n_heads
    d = q_ref.shape[-1]
    qrow = q_ref[0].astype(F32)
    row = lax.broadcasted_iota(jnp.int32, (rows, d), 0)
    col = lax.broadcasted_iota(jnp.int32, (rows, d), 1)
    own = (col // head_dim) == 2 * (row % n_heads) + row // n_heads
    return jnp.where(own, jnp.broadcast_to(qrow, (rows, d)), 0.0)


def _decode_start(q_ref, kn_ref, vn_ref, m_ref, l_ref, acc_ref, *, n_heads, head_dim):
    qm = _decode_query_rows(q_ref, n_heads=n_heads, head_dim=head_dim)
    m_ref[...] = jnp.sum(qm * kn_ref[0], axis=1, keepdims=True)
    l_ref[...] = jnp.ones(l_ref.shape, F32)
    acc_ref[...] = jnp.concatenate([vn_ref[0], vn_ref[0]], axis=0)


def _decode_finish(lam, sg_ref, o_ref, l_ref, acc_ref, *, n_heads, lam_init):
    w = acc_ref[...] / l_ref[...]
    out = w[0:n_heads] - lam * w[n_heads:2 * n_heads]
    o_ref[0] = (_rms_hat(out) * sg_ref[...] * (1.0 - lam_init)).astype(BF16)


def _decode_pages(q_ref, e_ref, k_refs, v_refs, m_ref, l_ref, acc_ref, *, n_heads, head_dim):
    pages = len(k_refs)
    rows = 2 * n_heads
    page = k_refs[0].shape[-1]
    vdim = v_refs[0].shape[-1]
    qm = _decode_query_rows(q_ref, n_heads=n_heads, head_dim=head_dim).astype(BF16)
    srow = lax.broadcasted_iota(jnp.int32, (rows, page * n_heads), 0)
    scol = lax.broadcasted_iota(jnp.int32, (rows, page * n_heads), 1)
    own_head = scol % n_heads == srow % n_heads

    pair = 2 if pages % 2 == 0 else 1

    def side_by_side(refs, g):
        return jnp.concatenate([refs[g + u][0].astype(BF16) for u in range(pair)], axis=1)

    s = jnp.concatenate([_dot(qm, side_by_side(k_refs, g)) for g in range(0, pages, pair)],
                        axis=1)
    m_old = m_ref[...]
    m_new = jnp.maximum(m_old, jnp.max(s, axis=1, keepdims=True))
    alpha = jnp.exp2(m_old - m_new)
    p = jnp.exp2(s - m_new).astype(BF16)
    l_ref[...] = alpha * l_ref[...] + jnp.sum(p.astype(F32), axis=1, keepdims=True)
    m_ref[...] = m_new
    p_stack = jnp.concatenate([p[:, g * page:(g + 1) * page] for g in range(pages)], axis=0)
    spread = _dot(p_stack, e_ref[...])
    pv = None
    for g in range(0, pages, pair):
        own = jnp.concatenate([own_head] * pair, axis=0)
        p_exp = jnp.where(own, spread[g * rows:(g + pair) * rows], 0.0).astype(BF16)
        r = _dot(p_exp, side_by_side(v_refs, g))
        for u in range(pair):
            term = r[u * rows:(u + 1) * rows, u * vdim:(u + 1) * vdim]
            pv = term if pv is None else pv + term
    acc_ref[...] = alpha * acc_ref[...] + pv


def _attention_kernel(pt_ref, lq1_ref, lk1_ref, lq2_ref, lk2_ref, sgc_ref, sgr_ref,
                      q_ref, k_ref, vt_ref, dq_ref, kn_ref, vn_ref, e_ref, *rest,
                      pages, n_dsteps, blk, n_heads, head_dim, vdim, lam_init):
    del pt_ref
    k_refs = rest[:pages]
    v_refs = rest[pages:2 * pages]
    o_ref, od_ref, m_ref, acc_ref, sa_ref, sb_ref, dm_ref, dl_ref, dacc_ref = rest[2 * pages:]
    lam = _lambda_value(lq1_ref, lk1_ref, lq2_ref, lk2_ref, lam_init)
    step = (pl.program_id(0) * pl.num_programs(1) + pl.program_id(1)) * pl.num_programs(2) \
        + pl.program_id(2)
    dstep = step % n_dsteps

    @pl.when(dstep == 0)
    def _():
        _decode_start(dq_ref, kn_ref, vn_ref, dm_ref, dl_ref, dacc_ref,
                      n_heads=n_heads, head_dim=head_dim)

    decode = functools.partial(_decode_pages, dq_ref, e_ref, k_refs, v_refs, dm_ref, dl_ref,
                               dacc_ref, n_heads=n_heads, head_dim=head_dim)
    _prompt_block(lam, sgc_ref, q_ref, k_ref, vt_ref, o_ref, m_ref, acc_ref, sa_ref, sb_ref,
                  qi=pl.program_id(2), blk=blk, head_dim=head_dim, vdim=vdim, lam_init=lam_init,
                  alongside_first_scores=decode)

    @pl.when(dstep == n_dsteps - 1)
    def _():
        _decode_finish(lam, sgr_ref, od_ref, dl_ref, dacc_ref, n_heads=n_heads, lam_init=lam_init)


def _attention(q, k, vt, q_s, k_new, v_new, cache_k, cache_v, page_table, lam_vecs, sub_g,
               n_heads, head_dim, lam_init):
    b, s, d = q.shape
    blk = vt.shape[-1]
    nq = s // blk
    vdim = d // n_heads
    va = vdim + VT_ONES_ROWS
    assert vdim == LANES and 2 * head_dim == LANES and vt.shape[2] == n_heads * va
    n = q_s.shape[0]
    n_pool, page = cache_k.shape[:2]
    n_pages = page_table.shape[1]
    steps = b * n_heads * nq
    assert steps % n == 0 and n_pages % (steps // n) == 0
    n_dsteps = steps // n
    pages = n_pages // n_dsteps
    rows = 2 * n_heads
    kt = jnp.transpose(cache_k, (0, 2, 3, 4, 1)).reshape(n_pool, d, page)
    v2 = cache_v.reshape(n_pool, page * n_heads, vdim)
    spread = (jnp.arange(page * n_heads)[None, :] // n_heads
              == jnp.arange(page)[:, None]).astype(BF16)

    def lin(i, h, t):
        return (i * n_heads + h) * nq + t

    def seq_of(i, h, t):
        return lin(i, h, t) // n_dsteps

    def page_of(g):
        return lambda i, h, t, pt: (pt[lin(i, h, t), g], 0, 0)

    const = lambda shape: pl.BlockSpec(shape, lambda i, h, t, pt: (0,) * len(shape))
    per_seq = lambda shape: pl.BlockSpec((1,) + shape, lambda i, h, t, pt: (seq_of(i, h, t), 0, 0))
    kern = functools.partial(_attention_kernel, pages=pages, n_dsteps=n_dsteps, blk=blk,
                             n_heads=n_heads, head_dim=head_dim, vdim=vdim, lam_init=lam_init)
    grid_spec = pltpu.PrefetchScalarGridSpec(
        num_scalar_prefetch=1,
        grid=(b, n_heads, nq),
        in_specs=[const((1, head_dim))] * 4 + [const((vdim, 1)), const((1, vdim)),
                  pl.BlockSpec((1, blk, LANES), lambda i, h, t, pt: (i, t, h)),
                  pl.BlockSpec((1, s, LANES), lambda i, h, t, pt: (i, 0, h)),
                  pl.BlockSpec((1, nq, va, blk), lambda i, h, t, pt: (i, 0, h, 0)),
                  per_seq((1, d)), per_seq((1, d)), per_seq((n_heads, vdim)),
                  const((page, page * n_heads))]
                 + [pl.BlockSpec((1, d, page), page_of(g)) for g in range(pages)]
                 + [pl.BlockSpec((1, page * n_heads, vdim), page_of(g)) for g in range(pages)],
        out_specs=(pl.BlockSpec((1, blk, LANES), lambda i, h, t, pt: (i, t, h)),
                   per_seq((n_heads, vdim))),
        scratch_shapes=[pltpu.VMEM((2, 1, blk), F32), pltpu.VMEM((2, va, blk), F32),
                        pltpu.VMEM((2, blk, blk), F32), pltpu.VMEM((2, blk, blk), F32),
                        pltpu.VMEM((rows, 1), F32), pltpu.VMEM((rows, 1), F32),
                        pltpu.VMEM((rows, vdim), F32)],
    )
    o, o_s = pl.pallas_call(
        kern,
        out_shape=(jax.ShapeDtypeStruct((b, s, d), BF16),
                   jax.ShapeDtypeStruct((n, n_heads, vdim), BF16)),
        grid_spec=grid_spec,
        compiler_params=_params("arbitrary", "arbitrary", "arbitrary"),
        name="diff_attention",
    )(page_table.reshape(steps, pages), *lam_vecs, sub_g.reshape(-1, 1), sub_g.reshape(1, -1), q, k, vt,
      q_s.reshape(n, 1, d), k_new.reshape(n, 1, d), v_new.reshape(n, n_heads, vdim),
      spread, *([kt] * pages), *([v2] * pages))
    return o, o_s.reshape(n, d)


def kernel(x_prompt, x_sample, p_prompt, p_sample, state_conv, cache_k, cache_v, page_table,
           a_norm, a_w_in, a_conv_w, a_conv_b, a_w_out, kv_norm, w_k, w_v, b_norm, b_wq,
           b_lq1, b_lk1, b_lq2, b_lk2, b_subln, b_wo, mlp_norm, w_up, w_down, ple_norm,
           w_ple_gate, w_ple_proj, final_norm):
    batch, seq, d = x_prompt.shape
    dec_batch, dec_seq, _ = x_sample.shape
    depth = p_prompt.shape[0]
    n_pool, page, n_heads, two, head_dim = cache_k.shape
    n_a = state_conv.shape[0]
    assert depth == 2 and n_a == 1 and two == 2 and dec_seq == 1
    past_len = page_table.shape[1] * page
    li_b = n_a
    lam_init = _lambda_init(li_b)

    bf = lambda w: w.astype(BF16)
    row = lambda g: g.reshape(1, -1).astype(F32)
    w_in, w_out = bf(a_w_in[0]), bf(a_w_out[0])
    wk, wv, wq, wo = bf(w_k), bf(w_v), bf(b_wq[0]), bf(b_wo[0])
    lam_vecs = tuple(row(v[0]) for v in (b_lq1, b_lk1, b_lq2, b_lk2))
    mlp_weights = [(row(mlp_norm[li]), bf(w_up[li]), bf(w_down[li]), row(ple_norm[li]),
                    bf(w_ple_gate[li]), bf(w_ple_proj[li])) for li in range(depth)]
    pp = p_prompt.reshape(depth, batch * seq, -1)
    ps = p_sample.reshape(depth, dec_batch, -1)

    zeros_state = jnp.zeros((batch, CONV_W - 1, d), F32)
    h, conv_p = _mixer_seq(x_prompt, zeros_state, row(a_norm[0]), w_in, a_conv_w[0],
                           row(a_conv_b[0]), w_out)
    h = _mlp_ple(h.reshape(batch * seq, d), pp, 0, *mlp_weights[0])
    tables_p = _rope_tables(jnp.arange(seq), head_dim)
    k_p, v_p, q_b, k_b, vt_b = _qkv(h.reshape(batch, seq, d), row(kv_norm), row(b_norm[0]),
                                    wk, wv, wq, tables_p, n_heads, head_dim, True)

    xs = x_sample.reshape(dec_batch, d)
    hs, u_s = _mixer_step(xs, state_conv[0, :, 0], state_conv[0, :, 1], row(a_norm[0]), w_in,
                          a_conv_w[0], row(a_conv_b[0]), w_out)
    conv_s = jnp.stack([state_conv[0, :, 1], u_s], axis=1)
    hs = _mlp_ple(hs, ps, 0, *mlp_weights[0])
    tables_s = _rope_tables(jnp.full((dec_batch,), past_len, jnp.int32), head_dim)
    k_s, v_s, q_s = _qkv(hs.reshape(1, dec_batch, d), row(kv_norm), row(b_norm[0]),
                         wk, wv, wq, tables_s, n_heads, head_dim, False)
    k_s, v_s, q_s = (a.reshape(dec_batch, d) for a in (k_s, v_s, q_s))

    o, o_s = _attention(q_b, k_b, vt_b, q_s, k_s, v_s, cache_k, cache_v, page_table, lam_vecs,
                        b_subln[0].astype(F32), n_heads, head_dim, lam_init)
    y_prompt = _mlp_ple(h, pp, 1, *mlp_weights[1],
                        attn=(o.reshape(batch * seq, d), wo), final_g=row(final_norm))
    y_sample = _mlp_ple(hs, ps, 1, *mlp_weights[1],
                        attn=(o_s, wo), final_g=row(final_norm))

    return (y_prompt.reshape(batch, seq, d),
            y_sample.reshape(dec_batch, dec_seq, d),
            conv_p[None],
            conv_s[None],
            k_p.reshape(batch, seq, n_heads, 2, head_dim),
            v_p.reshape(batch, seq, n_heads, d // n_heads),
            k_s.reshape(dec_batch, dec_seq, n_heads, 2, head_dim),
            v_s.reshape(dec_batch, dec_seq, n_heads, d // n_heads))
```

```python
import functools
import math

import jax
import jax.numpy as jnp
from jax import lax
from jax.experimental import pallas as pl
from jax.experimental.pallas import tpu as pltpu

F32 = jnp.float32
BF16 = jnp.bfloat16

EPS = 1e-6
NEG = -1e30
ROPE_THETA = 500000.0
CONV_W = 3

V7X_VMEM_LIMIT_BYTES = 56 * 1024 * 1024
LANES = 128

TOKEN_TILE = 512
FF_CHUNK = 1024
VT_ONES_ROWS = 16
LOG2E = math.log2(math.e)
KEY_LOOP_UNROLL = 8


def _lambda_init(layer_idx):
    return 0.8 - 0.6 * math.exp(-0.3 * layer_idx)


def _rms_hat(x):
    return x * lax.rsqrt(jnp.mean(x * x, axis=-1, keepdims=True) + EPS)


def _dot(a, b):
    return jnp.dot(a, b, preferred_element_type=F32)


def _dot_nt(a, b):
    return lax.dot_general(a, b, (((1,), (1,)), ((), ())), preferred_element_type=F32)


def _resident(shape):
    nd = len(shape)
    return pl.BlockSpec(shape, lambda *_: (0,) * nd, pipeline_mode=pl.Buffered(1))


def _params(*sem):
    return pltpu.CompilerParams(dimension_semantics=sem, vmem_limit_bytes=V7X_VMEM_LIMIT_BYTES)


def _mixer_seq_kernel(x_ref, st_ref, g_ref, win_ref, cw_ref, cb_ref, wout_ref,
                      h_ref, stout_ref, ubuf, *, tm, d):
    @pl.when(pl.program_id(1) == 0)
    def _():
        ubuf[6:8, :] = st_ref[0]

    x = x_ref[0]
    hn = (_rms_hat(x) * g_ref[...]).astype(BF16)
    gc = _dot(hn, win_ref[:, d:2 * d])
    xin = _dot(hn, win_ref[:, 2 * d:3 * d])
    u = gc * xin
    ubuf[8:8 + tm, :] = u
    conv = (cb_ref[...] + ubuf[6:6 + tm, :] * cw_ref[0:1, :]
            + ubuf[7:7 + tm, :] * cw_ref[1:2, :] + u * cw_ref[2:3, :])
    gb = _dot(hn, win_ref[:, 0:d])
    y = _dot((gb * conv).astype(BF16), wout_ref[...])
    h_ref[0] = x + y
    last = ubuf[6 + tm:8 + tm, :]
    stout_ref[0] = last
    ubuf[6:8, :] = last


def _mixer_seq(x, state, g, w_in, conv_w, conv_b, w_out):
    b, s, d = x.shape
    tm = min(TOKEN_TILE, s)
    assert s % tm == 0 and tm >= CONV_W - 1
    kern = functools.partial(_mixer_seq_kernel, tm=tm, d=d)
    return pl.pallas_call(
        kern,
        out_shape=(jax.ShapeDtypeStruct((b, s, d), F32),
                   jax.ShapeDtypeStruct((b, CONV_W - 1, d), F32)),
        grid=(b, s // tm),
        in_specs=[
            pl.BlockSpec((1, tm, d), lambda i, t: (i, t, 0)),
            pl.BlockSpec((1, CONV_W - 1, d), lambda i, t: (i, 0, 0)),
            _resident((1, d)),
            _resident((d, 3 * d)),
            _resident((CONV_W, d)),
            _resident((1, d)),
            _resident((d, d)),
        ],
        out_specs=(pl.BlockSpec((1, tm, d), lambda i, t: (i, t, 0)),
                   pl.BlockSpec((1, CONV_W - 1, d), lambda i, t: (i, 0, 0))),
        scratch_shapes=[pltpu.VMEM((tm + 8, d), F32)],
        compiler_params=_params("arbitrary", "arbitrary"),
        name="mixer_seq",
    )(x, state, g, w_in, conv_w, conv_b, w_out)


def _mixer_step_kernel(x_ref, s0_ref, s1_ref, g_ref, win_ref, cw_ref, cb_ref, wout_ref,
                       h_ref, u_ref, *, d):
    x = x_ref[...]
    hn = (_rms_hat(x) * g_ref[...]).astype(BF16)
    gc = _dot(hn, win_ref[:, d:2 * d])
    xin = _dot(hn, win_ref[:, 2 * d:3 * d])
    u = gc * xin
    conv = (cb_ref[...] + s0_ref[...] * cw_ref[0:1, :]
            + s1_ref[...] * cw_ref[1:2, :] + u * cw_ref[2:3, :])
    gb = _dot(hn, win_ref[:, 0:d])
    y = _dot((gb * conv).astype(BF16), wout_ref[...])
    h_ref[...] = x + y
    u_ref[...] = u


def _mixer_step(x, s0, s1, g, w_in, conv_w, conv_b, w_out):
    n, d = x.shape
    kern = functools.partial(_mixer_step_kernel, d=d)
    return pl.pallas_call(
        kern,
        out_shape=(jax.ShapeDtypeStruct((n, d), F32), jax.ShapeDtypeStruct((n, d), F32)),
        compiler_params=pltpu.CompilerParams(vmem_limit_bytes=V7X_VMEM_LIMIT_BYTES),
        name="mixer_step",
    )(x, s0, s1, g, w_in, conv_w, conv_b, w_out)


def _mlp_ple_kernel(*refs, has_attn, has_final, ff, ff_chunk):
    refs = list(refs)
    h_ref = refs.pop(0)
    if has_attn:
        o_ref = refs.pop(0)
        wo_ref = refs.pop(0)
    mg_ref, wup_ref, wdown_ref, pg_ref, wgate_ref, p_ref, wproj_ref = refs[:7]
    refs = refs[7:]
    if has_final:
        fg_ref = refs.pop(0)
    out_ref = refs.pop(0)

    h = h_ref[...]
    if has_attn:
        h = h + _dot(o_ref[...], wo_ref[...])
    m = (_rms_hat(h) * mg_ref[...]).astype(BF16)
    acc = h
    for c in range(ff // ff_chunk):
        lo = c * ff_chunk
        a = _dot(m, wup_ref[:, lo:lo + ff_chunk])
        a = jnp.square(jnp.maximum(a, 0.0)).astype(BF16)
        acc = acc + _dot(a, wdown_ref[lo:lo + ff_chunk, :])
    gate = jax.nn.sigmoid(_dot((_rms_hat(acc) * pg_ref[...]).astype(BF16), wgate_ref[...]))
    h3 = acc + gate * _dot(p_ref[...].astype(BF16), wproj_ref[...])
    if has_final:
        h3 = _rms_hat(h3) * fg_ref[...]
    out_ref[...] = h3


def _mlp_ple(h, p_layers, layer, mg, w_up, w_down, pg, w_gate, w_proj, attn=None, final_g=None):
    n, d = h.shape
    ff = w_up.shape[1]
    ple = p_layers.shape[2]
    tm = min(TOKEN_TILE, n)
    assert n % tm == 0
    ff_chunk = min(FF_CHUNK, ff)
    assert ff % ff_chunk == 0
    row = lambda i: (i, 0)
    args = [h]
    specs = [pl.BlockSpec((tm, d), row)]
    if attn is not None:
        o, wo = attn
        args += [o, wo]
        specs += [pl.BlockSpec((tm, o.shape[1]), row), _resident(wo.shape)]
    args += [mg, w_up, w_down, pg, w_gate, p_layers, w_proj]
    specs += [_resident((1, d)), _resident((d, ff)), _resident((ff, d)), _resident((1, d)),
              _resident((d, d)), pl.BlockSpec((None, tm, ple), lambda i: (layer, i, 0)),
              _resident((ple, d))]
    if final_g is not None:
        args.append(final_g)
        specs.append(_resident((1, d)))
    kern = functools.partial(_mlp_ple_kernel, has_attn=attn is not None,
                             has_final=final_g is not None, ff=ff, ff_chunk=ff_chunk)
    return pl.pallas_call(
        kern,
        out_shape=jax.ShapeDtypeStruct((n, d), F32),
        grid=(n // tm,),
        in_specs=specs,
        out_specs=pl.BlockSpec((tm, d), row),
        compiler_params=_params("arbitrary"),
        name="mlp_ple",
    )(*args)


def _rope_tables(pos, head_dim):
    rope_dim = head_dim // 4
    half = rope_dim // 2
    inv = jnp.power(jnp.float32(ROPE_THETA),
                    -jnp.arange(0, rope_dim, 2, dtype=jnp.float32) / rope_dim)
    ang = pos.astype(jnp.float32)[:, None] * inv[None, :]
    cos = jnp.cos(ang)
    sin = jnp.sin(ang)
    dcol = jnp.arange(2 * head_dim) % head_dim
    j = dcol % half
    c = jnp.where(dcol[None, :] < rope_dim, cos[:, j], 1.0)
    sa = jnp.where(dcol[None, :] < half, -sin[:, j], 0.0)
    sb = jnp.where((dcol[None, :] >= half) & (dcol[None, :] < rope_dim), sin[:, j], 0.0)
    return c.astype(F32), sa.astype(F32), sb.astype(F32)


def _rope_head(t, c, sa, sb, half):
    width = t.shape[-1]
    return t * c + pltpu.roll(t, width - half, 1) * sa + pltpu.roll(t, half, 1) * sb


def _qkv_kernel(h_ref, kvg_ref, qg_ref, wk_ref, wv_ref, wq_ref, c_ref, sa_ref, sb_ref,
                k_ref, v_ref, qb_ref, *attn_refs, n_heads, half, q_scale):
    tm = h_ref.shape[1]
    h = h_ref[0]
    xh = _rms_hat(h)
    hk = (xh * kvg_ref[...]).astype(BF16)
    hq = (xh * qg_ref[...]).astype(BF16)
    c = c_ref[...]
    sa = sa_ref[...]
    sb = sb_ref[...]
    k = _dot(hk, wk_ref[...])
    q = _dot(hq, wq_ref[...])
    v = _dot(hk, wv_ref[...])
    v_ref[0] = v
    for hd in range(n_heads):
        cols = slice(hd * LANES, (hd + 1) * LANES)
        kh = _rope_head(k[:, cols], c, sa, sb, half)
        k_ref[0, :, cols] = kh
        qb_ref[0, :, cols] = (_rope_head(q[:, cols], c, sa, sb, half) * q_scale).astype(BF16)
        if attn_refs:
            attn_refs[0][0, :, cols] = kh.astype(BF16)
    if attn_refs:
        vt = v.T.astype(BF16)
        va = LANES + VT_ONES_ROWS
        for hd in range(n_heads):
            attn_refs[1][0, 0, hd * va:hd * va + LANES, :] = vt[hd * LANES:(hd + 1) * LANES, :]
            attn_refs[1][0, 0, hd * va + LANES:(hd + 1) * va, :] = jnp.ones((VT_ONES_ROWS, tm), BF16)


def _qkv(h, kvg, qg, w_k, w_v, w_q, tables, n_heads, head_dim, for_attention):
    b, s, d = h.shape
    tm = min(TOKEN_TILE, s)
    assert s % tm == 0 and 2 * head_dim == LANES
    nt = s // tm
    tok = lambda i, t: (i, t, 0)
    tab = pl.BlockSpec((tm, LANES), lambda i, t: (t, 0))
    out_shape = [jax.ShapeDtypeStruct((b, s, d), F32), jax.ShapeDtypeStruct((b, s, d), F32),
                 jax.ShapeDtypeStruct((b, s, d), BF16)]
    out_specs = [pl.BlockSpec((1, tm, d), tok)] * 3
    if for_attention:
        vt_rows = n_heads * (LANES + VT_ONES_ROWS)
        out_shape += [jax.ShapeDtypeStruct((b, s, d), BF16),
                      jax.ShapeDtypeStruct((b, nt, vt_rows, tm), BF16)]
        out_specs += [pl.BlockSpec((1, tm, d), tok),
                      pl.BlockSpec((1, 1, vt_rows, tm), lambda i, t: (i, t, 0, 0))]
    kern = functools.partial(_qkv_kernel, n_heads=n_heads, half=head_dim // 8,
                             q_scale=head_dim ** -0.5 * LOG2E)
    return pl.pallas_call(
        kern,
        out_shape=tuple(out_shape),
        grid=(b, nt),
        in_specs=[pl.BlockSpec((1, tm, d), tok), _resident((1, d)), _resident((1, d)),
                  _resident((d, d)), _resident((d, d)), _resident((d, d)), tab, tab, tab],
        out_specs=tuple(out_specs),
        compiler_params=_params("arbitrary", "arbitrary"),
        name="qkv_rope",
    )(h, kvg, qg, w_k, w_v, w_q, *tables)


def _lambda_value(lq1_ref, lk1_ref, lq2_ref, lk2_ref, lam_init):
    s1 = jnp.sum(lq1_ref[...] * lk1_ref[...], axis=-1, keepdims=True)
    s2 = jnp.sum(lq2_ref[...] * lk2_ref[...], axis=-1, keepdims=True)
    return jnp.exp(s1) - jnp.exp(s2) + lam_init


def _prompt_block(lam, sg_ref, q_ref, k_ref, vt_ref, o_ref, m_ref, acc_ref, sa_ref, sb_ref,
                  *, qi, blk, head_dim, vdim, lam_init, alongside_first_scores):
    q = q_ref[0]
    lane = lax.broadcasted_iota(jnp.int32, q.shape, 1)
    zero = jnp.zeros_like(q)
    q_comp = (jnp.where(lane < head_dim, q, zero), jnp.where(lane >= head_dim, q, zero))

    m_ref[...] = jnp.full(m_ref.shape, NEG, F32)
    acc_ref[...] = jnp.zeros(acc_ref.shape, F32)

    def scores(j, s_ref):
        start = pl.multiple_of(j * blk, blk)
        kb = k_ref[0, pl.ds(start, blk), :]
        for c in range(2):
            s_ref[c] = _dot_nt(kb, q_comp[c])

    def update(j, s_ref, masked):
        vt = vt_ref[0, j]
        for c in range(2):
            s = s_ref[c]
            if masked:
                kpos = lax.broadcasted_iota(jnp.int32, s.shape, 0)
                qpos = lax.broadcasted_iota(jnp.int32, s.shape, 1)
                s = jnp.where(kpos <= qpos, s, NEG)
            m_old = m_ref[c]
            m_new = jnp.maximum(m_old, jnp.max(s, axis=0, keepdims=True))
            alpha = jnp.exp2(m_old - m_new)
            p = jnp.exp2(s - m_new).astype(BF16)
            acc_ref[c] = alpha * acc_ref[c] + _dot(vt, p)
            m_ref[c] = m_new

    scores(0, sa_ref)
    alongside_first_scores()

    bufs = (sa_ref, sb_ref)

    def run(j0, n_unmasked, masked_last):
        total = n_unmasked + (1 if masked_last else 0)
        for u in range(total):
            if u + 1 < total or not masked_last:
                scores(j0 + u + 1, bufs[(u + 1) % 2])
            update(j0 + u, bufs[u % 2], masked_last and u == total - 1)

    def body(i, carry):
        run(KEY_LOOP_UNROLL * i, KEY_LOOP_UNROLL, False)
        return carry

    lax.fori_loop(0, qi // KEY_LOOP_UNROLL, body, 0)
    j_rem = (qi // KEY_LOOP_UNROLL) * KEY_LOOP_UNROLL
    def finish():
        inv1 = 1.0 / acc_ref[0, vdim:vdim + 1, :]
        inv2 = lam / acc_ref[1, vdim:vdim + 1, :]
        out = acc_ref[0, 0:vdim, :] * inv1 - acc_ref[1, 0:vdim, :] * inv2
        ms = jnp.mean(out * out, axis=0, keepdims=True)
        y = out * lax.rsqrt(ms + EPS) * sg_ref[...] * (1.0 - lam_init)
        o_ref[0] = y.T.astype(BF16)

    for r in range(KEY_LOOP_UNROLL):
        @pl.when(qi % KEY_LOOP_UNROLL == r)
        def _(r=r):
            run(j_rem, r, True)
            finish()


def _decode_query_rows(q_ref, *, n_heads, head_dim):
    rows = 2 * n_heads
    d = q_ref.shape[-1]
    qrow = q_ref[0].astype(F32)
    row = lax.broadcasted_iota(jnp.int32, (rows, d), 0)
    col = lax.broadcasted_iota(jnp.int32, (rows, d), 1)
    own = (col // head_dim) == 2 * (row % n_heads) + row // n_heads
    return jnp.where(own, jnp.broadcast_to(qrow, (rows, d)), 0.0)


def _decode_start(q_ref, kn_ref, vn_ref, m_ref, l_ref, acc_ref, *, n_heads, head_dim):
    qm = _decode_query_rows(q_ref, n_heads=n_heads, head_dim=head_dim)
    m_ref[...] = jnp.sum(qm * kn_ref[0], axis=1, keepdims=True)
    l_ref[...] = jnp.ones(l_ref.shape, F32)
    acc_ref[...] = jnp.concatenate([vn_ref[0], vn_ref[0]], axis=0)


def _decode_finish(lam, sg_ref, o_ref, l_ref, acc_ref, *, n_heads, lam_init):
    w = acc_ref[...] / l_ref[...]
    out = w[0:n_heads] - lam * w[n_heads:2 * n_heads]
    o_ref[0] = (_rms_hat(out) * sg_ref[...] * (1.0 - lam_init)).astype(BF16)


def _decode_pages(q_ref, e_ref, k_refs, v_refs, m_ref, l_ref, acc_ref, *, n_heads, head_dim):
    pages = len(k_refs)
    rows = 2 * n_heads
    page = k_refs[0].shape[-1]
    vdim = v_refs[0].shape[-1]
    qm = _decode_query_rows(q_ref, n_heads=n_heads, head_dim=head_dim).astype(BF16)
    srow = lax.broadcasted_iota(jnp.int32, (rows, page * n_heads), 0)
    scol = lax.broadcasted_iota(jnp.int32, (rows, page * n_heads), 1)
    own_head = scol % n_heads == srow % n_heads

    pair = 2 if pages % 2 == 0 else 1

    def side_by_side(refs, g):
        return jnp.concatenate([refs[g + u][0].astype(BF16) for u in range(pair)], axis=1)

    s = jnp.concatenate([_dot(qm, side_by_side(k_refs, g)) for g in range(0, pages, pair)],
                        axis=1)
    m_old = m_ref[...]
    m_new = jnp.maximum(m_old, jnp.max(s, axis=1, keepdims=True))
    alpha = jnp.exp2(m_old - m_new)
    p = jnp.exp2(s - m_new).astype(BF16)
    l_ref[...] = alpha * l_ref[...] + jnp.sum(p.astype(F32), axis=1, keepdims=True)
    m_ref[...] = m_new
    p_stack = jnp.concatenate([p[:, g * page:(g + 1) * page] for g in range(pages)], axis=0)
    spread = _dot(p_stack, e_ref[...])
    pv = None
    for g in range(0, pages, pair):
        own = jnp.concatenate([own_head] * pair, axis=0)
        p_exp = jnp.where(own, spread[g * rows:(g + pair) * rows], 0.0).astype(BF16)
        r = _dot(p_exp, side_by_side(v_refs, g))
        for u in range(pair):
            term = r[u * rows:(u + 1) * rows, u * vdim:(u + 1) * vdim]
            pv = term if pv is None else pv + term
    acc_ref[...] = alpha * acc_ref[...] + pv


def _attention_kernel(pt_ref, lq1_ref, lk1_ref, lq2_ref, lk2_ref, sgc_ref, sgr_ref,
                      q_ref, k_ref, vt_ref, dq_ref, kn_ref, vn_ref, e_ref, *rest,
                      pages, n_dsteps, blk, n_heads, head_dim, vdim, lam_init):
    del pt_ref
    k_refs = rest[:pages]
    v_refs = rest[pages:2 * pages]
    o_ref, od_ref, m_ref, acc_ref, sa_ref, sb_ref, dm_ref, dl_ref, dacc_ref = rest[2 * pages:]
    lam = _lambda_value(lq1_ref, lk1_ref, lq2_ref, lk2_ref, lam_init)
    step = (pl.program_id(0) * pl.num_programs(1) + pl.program_id(1)) * pl.num_programs(2) \
        + pl.program_id(2)
    dstep = step % n_dsteps

    @pl.when(dstep == 0)
    def _():
        _decode_start(dq_ref, kn_ref, vn_ref, dm_ref, dl_ref, dacc_ref,
                      n_heads=n_heads, head_dim=head_dim)

    decode = functools.partial(_decode_pages, dq_ref, e_ref, k_refs, v_refs, dm_ref, dl_ref,
                               dacc_ref, n_heads=n_heads, head_dim=head_dim)
    _prompt_block(lam, sgc_ref, q_ref, k_ref, vt_ref, o_ref, m_ref, acc_ref, sa_ref, sb_ref,
                  qi=pl.program_id(2), blk=blk, head_dim=head_dim, vdim=vdim, lam_init=lam_init,
                  alongside_first_scores=decode)

    @pl.when(dstep == n_dsteps - 1)
    def _():
        _decode_finish(lam, sgr_ref, od_ref, dl_ref, dacc_ref, n_heads=n_heads, lam_init=lam_init)


def _attention(q, k, vt, q_s, k_new, v_new, cache_k, cache_v, page_table, lam_vecs, sub_g,
               n_heads, head_dim, lam_init):
    b, s, d = q.shape
    blk = vt.shape[-1]
    nq = s // blk
    vdim = d // n_heads
    va = vdim + VT_ONES_ROWS
    assert vdim == LANES and 2 * head_dim == LANES and vt.shape[2] == n_heads * va
    n = q_s.shape[0]
    n_pool, page = cache_k.shape[:2]
    n_pages = page_table.shape[1]
    steps = b * n_heads * nq
    assert steps % n == 0 and n_pages % (steps // n) == 0
    n_dsteps = steps // n
    pages = n_pages // n_dsteps
    rows = 2 * n_heads
    kt = jnp.transpose(cache_k, (0, 2, 3, 4, 1)).reshape(n_pool, d, page)
    v2 = cache_v.reshape(n_pool, page * n_heads, vdim)
    spread = (jnp.arange(page * n_heads)[None, :] // n_heads
              == jnp.arange(page)[:, None]).astype(BF16)

    def lin(i, h, t):
        return (i * n_heads + h) * nq + t

    def seq_of(i, h, t):
        return lin(i, h, t) // n_dsteps

    def page_of(g):
        return lambda i, h, t, pt: (pt[lin(i, h, t), g], 0, 0)

    const = lambda shape: pl.BlockSpec(shape, lambda i, h, t, pt: (0,) * len(shape))
    per_seq = lambda shape: pl.BlockSpec((1,) + shape, lambda i, h, t, pt: (seq_of(i, h, t), 0, 0))
    kern = functools.partial(_attention_kernel, pages=pages, n_dsteps=n_dsteps, blk=blk,
                             n_heads=n_heads, head_dim=head_dim, vdim=vdim, lam_init=lam_init)
    grid_spec = pltpu.PrefetchScalarGridSpec(
        num_scalar_prefetch=1,
        grid=(b, n_heads, nq),
        in_specs=[const((1, head_dim))] * 4 + [const((vdim, 1)), const((1, vdim)),
                  pl.BlockSpec((1, blk, LANES), lambda i, h, t, pt: (i, t, h)),
                  pl.BlockSpec((1, s, LANES), lambda i, h, t, pt: (i, 0, h)),
                  pl.BlockSpec((1, nq, va, blk), lambda i, h, t, pt: (i, 0, h, 0)),
                  per_seq((1, d)), per_seq((1, d)), per_seq((n_heads, vdim)),
                  const((page, page * n_heads))]
                 + [pl.BlockSpec((1, d, page), page_of(g)) for g in range(pages)]
                 + [pl.BlockSpec((1, page * n_heads, vdim), page_of(g)) for g in range(pages)],
        out_specs=(pl.BlockSpec((1, blk, LANES), lambda i, h, t, pt: (i, t, h)),
                   per_seq((n_heads, vdim))),
        scratch_shapes=[pltpu.VMEM((2, 1, blk), F32), pltpu.VMEM((2, va, blk), F32),
                        pltpu.VMEM((2, blk, blk), F32), pltpu.VMEM((2, blk, blk), F32),
                        pltpu.VMEM((rows, 1), F32), pltpu.VMEM((rows, 1), F32),
                        pltpu.VMEM((rows, vdim), F32)],
    )
    o, o_s = pl.pallas_call(
        kern,
        out_shape=(jax.ShapeDtypeStruct((b, s, d), BF16),
                   jax.ShapeDtypeStruct((n, n_heads, vdim), BF16)),
        grid_spec=grid_spec,
        compiler_params=_params("arbitrary", "arbitrary", "arbitrary"),
        name="diff_attention",
    )(page_table.reshape(steps, pages), *lam_vecs, sub_g.reshape(-1, 1), sub_g.reshape(1, -1), q, k, vt,
      q_s.reshape(n, 1, d), k_new.reshape(n, 1, d), v_new.reshape(n, n_heads, vdim),
      spread, *([kt] * pages), *([v2] * pages))
    return o, o_s.reshape(n, d)


def kernel(x_prompt, x_sample, p_prompt, p_sample, state_conv, cache_k, cache_v, page_table,
           a_norm, a_w_in, a_conv_w, a_conv_b, a_w_out, kv_norm, w_k, w_v, b_norm, b_wq,
           b_lq1, b_lk1, b_lq2, b_lk2, b_subln, b_wo, mlp_norm, w_up, w_down, ple_norm,
           w_ple_gate, w_ple_proj, final_norm):
    batch, seq, d = x_prompt.shape
    dec_batch, dec_seq, _ = x_sample.shape
    depth = p_prompt.shape[0]
    n_pool, page, n_heads, two, head_dim = cache_k.shape
    n_a = state_conv.shape[0]
    assert depth == 2 and n_a == 1 and two == 2 and dec_seq == 1
    past_len = page_table.shape[1] * page
    li_b = n_a
    lam_init = _lambda_init(li_b)

    bf = lambda w: w.astype(BF16)
    row = lambda g: g.reshape(1, -1).astype(F32)
    w_in, w_out = bf(a_w_in[0]), bf(a_w_out[0])
    wk, wv, wq, wo = bf(w_k), bf(w_v), bf(b_wq[0]), bf(b_wo[0])
    lam_vecs = tuple(row(v[0]) for v in (b_lq1, b_lk1, b_lq2, b_lk2))
    mlp_weights = [(row(mlp_norm[li]), bf(w_up[li]), bf(w_down[li]), row(ple_norm[li]),
                    bf(w_ple_gate[li]), bf(w_ple_proj[li])) for li in range(depth)]
    pp = p_prompt.reshape(depth, batch * seq, -1)
    ps = p_sample.reshape(depth, dec_batch, -1)

    zeros_state = jnp.zeros((batch, CONV_W - 1, d), F32)
    h, conv_p = _mixer_seq(x_prompt, zeros_state, row(a_norm[0]), w_in, a_conv_w[0],
                           row(a_conv_b[0]), w_out)
    h = _mlp_ple(h.reshape(batch * seq, d), pp, 0, *mlp_weights[0])
    tables_p = _rope_tables(jnp.arange(seq), head_dim)
    k_p, v_p, q_b, k_b, vt_b = _qkv(h.reshape(batch, seq, d), row(kv_norm), row(b_norm[0]),
                                    wk, wv, wq, tables_p, n_heads, head_dim, True)

    xs = x_sample.reshape(dec_batch, d)
    hs, u_s = _mixer_step(xs, state_conv[0, :, 0], state_conv[0, :, 1], row(a_norm[0]), w_in,
                          a_conv_w[0], row(a_conv_b[0]), w_out)
    conv_s = jnp.stack([state_conv[0, :, 1], u_s], axis=1)
    hs = _mlp_ple(hs, ps, 0, *mlp_weights[0])
    tables_s = _rope_tables(jnp.full((dec_batch,), past_len, jnp.int32), head_dim)
    k_s, v_s, q_s = _qkv(hs.reshape(1, dec_batch, d), row(kv_norm), row(b_norm[0]),
                         wk, wv, wq, tables_s, n_heads, head_dim, False)
    k_s, v_s, q_s = (a.reshape(dec_batch, d) for a in (k_s, v_s, q_s))

    o, o_s = _attention(q_b, k_b, vt_b, q_s, k_s, v_s, cache_k, cache_v, page_table, lam_vecs,
                        b_subln[0].astype(F32), n_heads, head_dim, lam_init)
    y_prompt = _mlp_ple(h, pp, 1, *mlp_weights[1],
                        attn=(o.reshape(batch * seq, d), wo), final_g=row(final_norm))
    y_sample = _mlp_ple(hs, ps, 1, *mlp_weights[1],
                        attn=(o_s, wo), final_g=row(final_norm))

    return (y_prompt.reshape(batch, seq, d),
            y_sample.reshape(dec_batch, dec_seq, d),
            conv_p[None],
            conv_s[None],
            k_p.reshape(batch, seq, n_heads, 2, head_dim),
            v_p.reshape(batch, seq, n_heads, d // n_heads),
            k_s.reshape(dec_batch, dec_seq, n_heads, 2, head_dim),
            v_s.reshape(dec_batch, dec_seq, n_heads, d // n_heads))
```

```python
import functools
import math

import jax
import jax.numpy as jnp
from jax import lax
from jax.experimental import pallas as pl
from jax.experimental.pallas import tpu as pltpu

F32 = jnp.float32
BF16 = jnp.bfloat16

EPS = 1e-6
NEG = -1e30
ROPE_THETA = 500000.0
CONV_W = 3

V7X_VMEM_LIMIT_BYTES = 56 * 1024 * 1024
LANES = 128

TOKEN_TILE = 512
FF_CHUNK = 1024
VT_ONES_ROWS = 16
LOG2E = math.log2(math.e)
MAX_QUERY_BLOCKS = 8


def _lambda_init(layer_idx):
    return 0.8 - 0.6 * math.exp(-0.3 * layer_idx)


def _rms_hat(x):
    return x * lax.rsqrt(jnp.mean(x * x, axis=-1, keepdims=True) + EPS)


def _dot(a, b):
    return jnp.dot(a, b, preferred_element_type=F32)


def _dot_nt(a, b):
    return lax.dot_general(a, b, (((1,), (1,)), ((), ())), preferred_element_type=F32)


def _resident(shape):
    nd = len(shape)
    return pl.BlockSpec(shape, lambda *_: (0,) * nd, pipeline_mode=pl.Buffered(1))


def _params(*sem):
    return pltpu.CompilerParams(dimension_semantics=sem, vmem_limit_bytes=V7X_VMEM_LIMIT_BYTES)


def _mixer_seq_kernel(x_ref, st_ref, g_ref, win_ref, cw_ref, cb_ref, wout_ref,
                      h_ref, stout_ref, ubuf, *, tm, d):
    @pl.when(pl.program_id(1) == 0)
    def _():
        ubuf[6:8, :] = st_ref[0]

    x = x_ref[0]
    hn = (_rms_hat(x) * g_ref[...]).astype(BF16)
    gc = _dot(hn, win_ref[:, d:2 * d])
    xin = _dot(hn, win_ref[:, 2 * d:3 * d])
    u = gc * xin
    ubuf[8:8 + tm, :] = u
    conv = (cb_ref[...] + ubuf[6:6 + tm, :] * cw_ref[0:1, :]
            + ubuf[7:7 + tm, :] * cw_ref[1:2, :] + u * cw_ref[2:3, :])
    gb = _dot(hn, win_ref[:, 0:d])
    y = _dot((gb * conv).astype(BF16), wout_ref[...])
    h_ref[0] = x + y
    last = ubuf[6 + tm:8 + tm, :]
    stout_ref[0] = last
    ubuf[6:8, :] = last


def _mixer_seq(x, state, g, w_in, conv_w, conv_b, w_out):
    b, s, d = x.shape
    tm = min(TOKEN_TILE, s)
    assert s % tm == 0 and tm >= CONV_W - 1
    kern = functools.partial(_mixer_seq_kernel, tm=tm, d=d)
    return pl.pallas_call(
        kern,
        out_shape=(jax.ShapeDtypeStruct((b, s, d), F32),
                   jax.ShapeDtypeStruct((b, CONV_W - 1, d), F32)),
        grid=(b, s // tm),
        in_specs=[
            pl.BlockSpec((1, tm, d), lambda i, t: (i, t, 0)),
            pl.BlockSpec((1, CONV_W - 1, d), lambda i, t: (i, 0, 0)),
            _resident((1, d)),
            _resident((d, 3 * d)),
            _resident((CONV_W, d)),
            _resident((1, d)),
            _resident((d, d)),
        ],
        out_specs=(pl.BlockSpec((1, tm, d), lambda i, t: (i, t, 0)),
                   pl.BlockSpec((1, CONV_W - 1, d), lambda i, t: (i, 0, 0))),
        scratch_shapes=[pltpu.VMEM((tm + 8, d), F32)],
        compiler_params=_params("arbitrary", "arbitrary"),
        name="mixer_seq",
    )(x, state, g, w_in, conv_w, conv_b, w_out)


def _mixer_step_kernel(x_ref, s0_ref, s1_ref, g_ref, win_ref, cw_ref, cb_ref, wout_ref,
                       h_ref, u_ref, *, d):
    x = x_ref[...]
    hn = (_rms_hat(x) * g_ref[...]).astype(BF16)
    gc = _dot(hn, win_ref[:, d:2 * d])
    xin = _dot(hn, win_ref[:, 2 * d:3 * d])
    u = gc * xin
    conv = (cb_ref[...] + s0_ref[...] * cw_ref[0:1, :]
            + s1_ref[...] * cw_ref[1:2, :] + u * cw_ref[2:3, :])
    gb = _dot(hn, win_ref[:, 0:d])
    y = _dot((gb * conv).astype(BF16), wout_ref[...])
    h_ref[...] = x + y
    u_ref[...] = u


def _mixer_step(x, s0, s1, g, w_in, conv_w, conv_b, w_out):
    n, d = x.shape
    kern = functools.partial(_mixer_step_kernel, d=d)
    return pl.pallas_call(
        kern,
        out_shape=(jax.ShapeDtypeStruct((n, d), F32), jax.ShapeDtypeStruct((n, d), F32)),
        compiler_params=pltpu.CompilerParams(vmem_limit_bytes=V7X_VMEM_LIMIT_BYTES),
        name="mixer_step",
    )(x, s0, s1, g, w_in, conv_w, conv_b, w_out)


def _mlp_ple_kernel(*refs, has_attn, has_final, ff, ff_chunk):
    refs = list(refs)
    h_ref = refs.pop(0)
    if has_attn:
        o_ref = refs.pop(0)
        wo_ref = refs.pop(0)
    mg_ref, wup_ref, wdown_ref, pg_ref, wgate_ref, p_ref, wproj_ref = refs[:7]
    refs = refs[7:]
    if has_final:
        fg_ref = refs.pop(0)
    out_ref = refs.pop(0)

    h = h_ref[...]
    if has_attn:
        h = h + _dot(o_ref[...], wo_ref[...])
    m = (_rms_hat(h) * mg_ref[...]).astype(BF16)
    acc = h
    for c in range(ff // ff_chunk):
        lo = c * ff_chunk
        a = _dot(m, wup_ref[:, lo:lo + ff_chunk])
        a = jnp.square(jnp.maximum(a, 0.0)).astype(BF16)
        acc = acc + _dot(a, wdown_ref[lo:lo + ff_chunk, :])
    gate = jax.nn.sigmoid(_dot((_rms_hat(acc) * pg_ref[...]).astype(BF16), wgate_ref[...]))
    h3 = acc + gate * _dot(p_ref[...].astype(BF16), wproj_ref[...])
    if has_final:
        h3 = _rms_hat(h3) * fg_ref[...]
    out_ref[...] = h3


def _mlp_ple(h, p_layers, layer, mg, w_up, w_down, pg, w_gate, w_proj, attn=None, final_g=None):
    n, d = h.shape
    ff = w_up.shape[1]
    ple = p_layers.shape[2]
    tm = min(TOKEN_TILE, n)
    assert n % tm == 0
    ff_chunk = min(FF_CHUNK, ff)
    assert ff % ff_chunk == 0
    row = lambda i: (i, 0)
    args = [h]
    specs = [pl.BlockSpec((tm, d), row)]
    if attn is not None:
        o, wo = attn
        args += [o, wo]
        specs += [pl.BlockSpec((tm, o.shape[1]), row), _resident(wo.shape)]
    args += [mg, w_up, w_down, pg, w_gate, p_layers, w_proj]
    specs += [_resident((1, d)), _resident((d, ff)), _resident((ff, d)), _resident((1, d)),
              _resident((d, d)), pl.BlockSpec((None, tm, ple), lambda i: (layer, i, 0)),
              _resident((ple, d))]
    if final_g is not None:
        args.append(final_g)
        specs.append(_resident((1, d)))
    kern = functools.partial(_mlp_ple_kernel, has_attn=attn is not None,
                             has_final=final_g is not None, ff=ff, ff_chunk=ff_chunk)
    return pl.pallas_call(
        kern,
        out_shape=jax.ShapeDtypeStruct((n, d), F32),
        grid=(n // tm,),
        in_specs=specs,
        out_specs=pl.BlockSpec((tm, d), row),
        compiler_params=_params("arbitrary"),
        name="mlp_ple",
    )(*args)


def _rope_tables(pos, head_dim):
    rope_dim = head_dim // 4
    half = rope_dim // 2
    inv = jnp.power(jnp.float32(ROPE_THETA),
                    -jnp.arange(0, rope_dim, 2, dtype=jnp.float32) / rope_dim)
    ang = pos.astype(jnp.float32)[:, None] * inv[None, :]
    cos = jnp.cos(ang)
    sin = jnp.sin(ang)
    dcol = jnp.arange(2 * head_dim) % head_dim
    j = dcol % half
    c = jnp.where(dcol[None, :] < rope_dim, cos[:, j], 1.0)
    sa = jnp.where(dcol[None, :] < half, -sin[:, j], 0.0)
    sb = jnp.where((dcol[None, :] >= half) & (dcol[None, :] < rope_dim), sin[:, j], 0.0)
    return c.astype(F32), sa.astype(F32), sb.astype(F32)


def _rope_head(t, c, sa, sb, half):
    width = t.shape[-1]
    return t * c + pltpu.roll(t, width - half, 1) * sa + pltpu.roll(t, half, 1) * sb


def _qkv_kernel(h_ref, kvg_ref, qg_ref, wk_ref, wv_ref, wq_ref, c_ref, sa_ref, sb_ref,
                k_ref, v_ref, qb_ref, *attn_refs, n_heads, half, q_scale):
    tm = h_ref.shape[1]
    h = h_ref[0]
    xh = _rms_hat(h)
    hk = (xh * kvg_ref[...]).astype(BF16)
    hq = (xh * qg_ref[...]).astype(BF16)
    c = c_ref[...]
    sa = sa_ref[...]
    sb = sb_ref[...]
    k = _dot(hk, wk_ref[...])
    q = _dot(hq, wq_ref[...])
    v = _dot(hk, wv_ref[...])
    v_ref[0] = v
    for hd in range(n_heads):
        cols = slice(hd * LANES, (hd + 1) * LANES)
        kh = _rope_head(k[:, cols], c, sa, sb, half)
        k_ref[0, :, cols] = kh
        qb_ref[0, :, cols] = (_rope_head(q[:, cols], c, sa, sb, half) * q_scale).astype(BF16)
        if attn_refs:
            attn_refs[0][0, :, cols] = kh.astype(BF16)
    if attn_refs:
        vt = v.T.astype(BF16)
        va = LANES + VT_ONES_ROWS
        for hd in range(n_heads):
            attn_refs[1][0, 0, hd * va:hd * va + LANES, :] = vt[hd * LANES:(hd + 1) * LANES, :]
            attn_refs[1][0, 0, hd * va + LANES:(hd + 1) * va, :] = jnp.ones((VT_ONES_ROWS, tm), BF16)


def _qkv(h, kvg, qg, w_k, w_v, w_q, tables, n_heads, head_dim, for_attention):
    b, s, d = h.shape
    tm = min(TOKEN_TILE, s)
    assert s % tm == 0 and 2 * head_dim == LANES
    nt = s // tm
    tok = lambda i, t: (i, t, 0)
    tab = pl.BlockSpec((tm, LANES), lambda i, t: (t, 0))
    out_shape = [jax.ShapeDtypeStruct((b, s, d), F32), jax.ShapeDtypeStruct((b, s, d), F32),
                 jax.ShapeDtypeStruct((b, s, d), BF16)]
    out_specs = [pl.BlockSpec((1, tm, d), tok)] * 3
    if for_attention:
        vt_rows = n_heads * (LANES + VT_ONES_ROWS)
        out_shape += [jax.ShapeDtypeStruct((b, s, d), BF16),
                      jax.ShapeDtypeStruct((b, nt, vt_rows, tm), BF16)]
        out_specs += [pl.BlockSpec((1, tm, d), tok),
                      pl.BlockSpec((1, 1, vt_rows, tm), lambda i, t: (i, t, 0, 0))]
    kern = functools.partial(_qkv_kernel, n_heads=n_heads, half=head_dim // 8,
                             q_scale=head_dim ** -0.5 * LOG2E)
    return pl.pallas_call(
        kern,
        out_shape=tuple(out_shape),
        grid=(b, nt),
        in_specs=[pl.BlockSpec((1, tm, d), tok), _resident((1, d)), _resident((1, d)),
                  _resident((d, d)), _resident((d, d)), _resident((d, d)), tab, tab, tab],
        out_specs=tuple(out_specs),
        compiler_params=_params("arbitrary", "arbitrary"),
        name="qkv_rope",
    )(h, kvg, qg, w_k, w_v, w_q, *tables)


def _lambda_value(lq1_ref, lk1_ref, lq2_ref, lk2_ref, lam_init):
    s1 = jnp.sum(lq1_ref[...] * lk1_ref[...], axis=-1, keepdims=True)
    s2 = jnp.sum(lq2_ref[...] * lk2_ref[...], axis=-1, keepdims=True)
    return jnp.exp(s1) - jnp.exp(s2) + lam_init


def _prompt_block(lam, sg_ref, q_ref, k_ref, vt_ref, o_ref, m_ref, acc_ref, sa_ref, sb_ref,
                  *, qi, n_query_blocks, blk, head_dim, vdim, lam_init, alongside):
    q = q_ref[0]
    lane = lax.broadcasted_iota(jnp.int32, q.shape, 1)
    zero = jnp.zeros_like(q)
    q_comp = (jnp.where(lane < head_dim, q, zero), jnp.where(lane >= head_dim, q, zero))

    m_ref[...] = jnp.full(m_ref.shape, NEG, F32)
    acc_ref[...] = jnp.zeros(acc_ref.shape, F32)

    def scores(j, s_ref):
        start = pl.multiple_of(j * blk, blk)
        kb = k_ref[0, pl.ds(start, blk), :]
        for c in range(2):
            s_ref[c] = _dot_nt(kb, q_comp[c])

    def update(j, s_ref, masked):
        vt = vt_ref[0, j]
        for c in range(2):
            s = s_ref[c]
            if masked:
                kpos = lax.broadcasted_iota(jnp.int32, s.shape, 0)
                qpos = lax.broadcasted_iota(jnp.int32, s.shape, 1)
                s = jnp.where(kpos <= qpos, s, NEG)
            m_old = m_ref[c]
            m_new = jnp.maximum(m_old, jnp.max(s, axis=0, keepdims=True))
            alpha = jnp.exp2(m_old - m_new)
            p = jnp.exp2(s - m_new).astype(BF16)
            acc_ref[c] = alpha * acc_ref[c] + _dot(vt, p)
            m_ref[c] = m_new

    bufs = (sa_ref, sb_ref)

    def run(j0, n_unmasked, masked_last):
        total = n_unmasked + (1 if masked_last else 0)
        for u in range(total):
            if u + 1 < total or not masked_last:
                scores(j0 + u + 1, bufs[(u + 1) % 2])
            update(j0 + u, bufs[u % 2], masked_last and u == total - 1)

    def finish():
        inv1 = 1.0 / acc_ref[0, vdim:vdim + 1, :]
        inv2 = lam / acc_ref[1, vdim:vdim + 1, :]
        out = acc_ref[0, 0:vdim, :] * inv1 - acc_ref[1, 0:vdim, :] * inv2
        ms = jnp.mean(out * out, axis=0, keepdims=True)
        y = out * lax.rsqrt(ms + EPS) * sg_ref[...] * (1.0 - lam_init)
        o_ref[0] = y.T.astype(BF16)

    for r in range(n_query_blocks):
        @pl.when(qi == r)
        def _(r=r):
            scores(0, sa_ref)
            alongside()
            run(0, r, True)
            finish()


def _decode_query_rows(q_ref, *, n_heads, head_dim):
    rows = 2 * n_heads
    d = q_ref.shape[-1]
    qrow = q_ref[0].astype(F32)
    row = lax.broadcasted_iota(jnp.int32, (rows, d), 0)
    col = lax.broadcasted_iota(jnp.int32, (rows, d), 1)
    own = (col // head_dim) == 2 * (row % n_heads) + row // n_heads
    return jnp.where(own, jnp.broadcast_to(qrow, (rows, d)), 0.0)


def _decode_start(q_ref, kn_ref, vn_ref, m_ref, l_ref, acc_ref, *, n_heads, head_dim):
    qm = _decode_query_rows(q_ref, n_heads=n_heads, head_dim=head_dim)
    m_ref[...] = jnp.sum(qm * kn_ref[0], axis=1, keepdims=True)
    l_ref[...] = jnp.ones(l_ref.shape, F32)
    acc_ref[...] = jnp.concatenate([vn_ref[0], vn_ref[0]], axis=0)


def _decode_finish(lam, sg_ref, o_ref, l_ref, acc_ref, *, n_heads, lam_init):
    w = acc_ref[...] / l_ref[...]
    out = w[0:n_heads] - lam * w[n_heads:2 * n_heads]
    o_ref[0] = (_rms_hat(out) * sg_ref[...] * (1.0 - lam_init)).astype(BF16)


def _decode_pages(q_ref, e_ref, k_refs, v_refs, m_ref, l_ref, acc_ref, *, n_heads, head_dim):
    pages = len(k_refs)
    rows = 2 * n_heads
    page = k_refs[0].shape[-1]
    vdim = v_refs[0].shape[-1]
    qm = _decode_query_rows(q_ref, n_heads=n_heads, head_dim=head_dim).astype(BF16)
    srow = lax.broadcasted_iota(jnp.int32, (rows, page * n_heads), 0)
    scol = lax.broadcasted_iota(jnp.int32, (rows, page * n_heads), 1)
    own_head = scol % n_heads == srow % n_heads

    pair = 2 if pages % 2 == 0 else 1

    def side_by_side(refs, g):
        return jnp.concatenate([refs[g + u][0].astype(BF16) for u in range(pair)], axis=1)

    s = jnp.concatenate([_dot(qm, side_by_side(k_refs, g)) for g in range(0, pages, pair)],
                        axis=1)
    m_old = m_ref[...]
    m_new = jnp.maximum(m_old, jnp.max(s, axis=1, keepdims=True))
    alpha = jnp.exp2(m_old - m_new)
    p = jnp.exp2(s - m_new).astype(BF16)
    l_ref[...] = alpha * l_ref[...] + jnp.sum(p.astype(F32), axis=1, keepdims=True)
    m_ref[...] = m_new
    p_stack = jnp.concatenate([p[:, g * page:(g + 1) * page] for g in range(pages)], axis=0)
    spread = _dot(p_stack, e_ref[...])
    pv = None
    for g in range(0, pages, pair):
        own = jnp.concatenate([own_head] * pair, axis=0)
        p_exp = jnp.where(own, spread[g * rows:(g + pair) * rows], 0.0).astype(BF16)
        r = _dot(p_exp, side_by_side(v_refs, g))
        for u in range(pair):
            term = r[u * rows:(u + 1) * rows, u * vdim:(u + 1) * vdim]
            pv = term if pv is None else pv + term
    acc_ref[...] = alpha * acc_ref[...] + pv


def _attention_kernel(pt_ref, lq1_ref, lk1_ref, lq2_ref, lk2_ref, sgc_ref, sgr_ref,
                      q_ref, k_ref, vt_ref, dq_ref, kn_ref, vn_ref, e_ref, *rest,
                      pages, n_dsteps, nq, blk, n_heads, head_dim, vdim, lam_init):
    del pt_ref
    k_refs = rest[:pages]
    v_refs = rest[pages:2 * pages]
    o_ref, od_ref, m_ref, acc_ref, sa_ref, sb_ref, dm_ref, dl_ref, dacc_ref = rest[2 * pages:]
    lam = _lambda_value(lq1_ref, lk1_ref, lq2_ref, lk2_ref, lam_init)
    step = (pl.program_id(0) * pl.num_programs(1) + pl.program_id(1)) * pl.num_programs(2) \
        + pl.program_id(2)
    dstep = step % n_dsteps

    @pl.when(dstep == 0)
    def _():
        _decode_start(dq_ref, kn_ref, vn_ref, dm_ref, dl_ref, dacc_ref,
                      n_heads=n_heads, head_dim=head_dim)

    decode = functools.partial(_decode_pages, dq_ref, e_ref, k_refs, v_refs, dm_ref, dl_ref,
                               dacc_ref, n_heads=n_heads, head_dim=head_dim)
    _prompt_block(lam, sgc_ref, q_ref, k_ref, vt_ref, o_ref, m_ref, acc_ref, sa_ref, sb_ref,
                  qi=pl.program_id(2), n_query_blocks=nq, blk=blk, head_dim=head_dim, vdim=vdim,
                  lam_init=lam_init, alongside=decode)

    @pl.when(dstep == n_dsteps - 1)
    def _():
        _decode_finish(lam, sgr_ref, od_ref, dl_ref, dacc_ref, n_heads=n_heads, lam_init=lam_init)


def _attention(q, k, vt, q_s, k_new, v_new, cache_k, cache_v, page_table, lam_vecs, sub_g,
               n_heads, head_dim, lam_init):
    b, s, d = q.shape
    blk = vt.shape[-1]
    nq = s // blk
    vdim = d // n_heads
    va = vdim + VT_ONES_ROWS
    assert vdim == LANES and 2 * head_dim == LANES and vt.shape[2] == n_heads * va
    assert nq <= MAX_QUERY_BLOCKS
    n = q_s.shape[0]
    n_pool, page = cache_k.shape[:2]
    n_pages = page_table.shape[1]
    steps = b * n_heads * nq
    assert steps % n == 0 and n_pages % (steps // n) == 0
    n_dsteps = steps // n
    pages = n_pages // n_dsteps
    rows = 2 * n_heads
    kt = jnp.transpose(cache_k, (0, 2, 3, 4, 1)).reshape(n_pool, d, page)
    v2 = cache_v.reshape(n_pool, page * n_heads, vdim)
    spread = (jnp.arange(page * n_heads)[None, :] // n_heads
              == jnp.arange(page)[:, None]).astype(BF16)

    def lin(i, h, t):
        return (i * n_heads + h) * nq + t

    def seq_of(i, h, t):
        return lin(i, h, t) // n_dsteps

    def page_of(g):
        return lambda i, h, t, pt: (pt[lin(i, h, t), g], 0, 0)

    const = lambda shape: pl.BlockSpec(shape, lambda i, h, t, pt: (0,) * len(shape))
    per_seq = lambda shape: pl.BlockSpec((1,) + shape, lambda i, h, t, pt: (seq_of(i, h, t), 0, 0))
    kern = functools.partial(_attention_kernel, pages=pages, n_dsteps=n_dsteps, nq=nq, blk=blk,
                             n_heads=n_heads, head_dim=head_dim, vdim=vdim, lam_init=lam_init)
    grid_spec = pltpu.PrefetchScalarGridSpec(
        num_scalar_prefetch=1,
        grid=(b, n_heads, nq),
        in_specs=[const((1, head_dim))] * 4 + [const((vdim, 1)), const((1, vdim)),
                  pl.BlockSpec((1, blk, LANES), lambda i, h, t, pt: (i, t, h)),
                  pl.BlockSpec((1, s, LANES), lambda i, h, t, pt: (i, 0, h)),
                  pl.BlockSpec((1, nq, va, blk), lambda i, h, t, pt: (i, 0, h, 0)),
                  per_seq((1, d)), per_seq((1, d)), per_seq((n_heads, vdim)),
                  const((page, page * n_heads))]
                 + [pl.BlockSpec((1, d, page), page_of(g)) for g in range(pages)]
                 + [pl.BlockSpec((1, page * n_heads, vdim), page_of(g)) for g in range(pages)],
        out_specs=(pl.BlockSpec((1, blk, LANES), lambda i, h, t, pt: (i, t, h)),
                   per_seq((n_heads, vdim))),
        scratch_shapes=[pltpu.VMEM((2, 1, blk), F32), pltpu.VMEM((2, va, blk), F32),
                        pltpu.VMEM((2, blk, blk), F32), pltpu.VMEM((2, blk, blk), F32),
                        pltpu.VMEM((rows, 1), F32), pltpu.VMEM((rows, 1), F32),
                        pltpu.VMEM((rows, vdim), F32)],
    )
    o, o_s = pl.pallas_call(
        kern,
        out_shape=(jax.ShapeDtypeStruct((b, s, d), BF16),
                   jax.ShapeDtypeStruct((n, n_heads, vdim), BF16)),
        grid_spec=grid_spec,
        compiler_params=_params("arbitrary", "arbitrary", "arbitrary"),
        name="diff_attention",
    )(page_table.reshape(steps, pages), *lam_vecs, sub_g.reshape(-1, 1), sub_g.reshape(1, -1), q, k, vt,
      q_s.reshape(n, 1, d), k_new.reshape(n, 1, d), v_new.reshape(n, n_heads, vdim),
      spread, *([kt] * pages), *([v2] * pages))
    return o, o_s.reshape(n, d)


def kernel(x_prompt, x_sample, p_prompt, p_sample, state_conv, cache_k, cache_v, page_table,
           a_norm, a_w_in, a_conv_w, a_conv_b, a_w_out, kv_norm, w_k, w_v, b_norm, b_wq,
           b_lq1, b_lk1, b_lq2, b_lk2, b_subln, b_wo, mlp_norm, w_up, w_down, ple_norm,
           w_ple_gate, w_ple_proj, final_norm):
    batch, seq, d = x_prompt.shape
    dec_batch, dec_seq, _ = x_sample.shape
    depth = p_prompt.shape[0]
    n_pool, page, n_heads, two, head_dim = cache_k.shape
    n_a = state_conv.shape[0]
    assert depth == 2 and n_a == 1 and two == 2 and dec_seq == 1
    past_len = page_table.shape[1] * page
    li_b = n_a
    lam_init = _lambda_init(li_b)

    bf = lambda w: w.astype(BF16)
    row = lambda g: g.reshape(1, -1).astype(F32)
    w_in, w_out = bf(a_w_in[0]), bf(a_w_out[0])
    wk, wv, wq, wo = bf(w_k), bf(w_v), bf(b_wq[0]), bf(b_wo[0])
    lam_vecs = tuple(row(v[0]) for v in (b_lq1, b_lk1, b_lq2, b_lk2))
    mlp_weights = [(row(mlp_norm[li]), bf(w_up[li]), bf(w_down[li]), row(ple_norm[li]),
                    bf(w_ple_gate[li]), bf(w_ple_proj[li])) for li in range(depth)]
    pp = p_prompt.reshape(depth, batch * seq, -1)
    ps = p_sample.reshape(depth, dec_batch, -1)

    zeros_state = jnp.zeros((batch, CONV_W - 1, d), F32)
    h, conv_p = _mixer_seq(x_prompt, zeros_state, row(a_norm[0]), w_in, a_conv_w[0],
                           row(a_conv_b[0]), w_out)
    h = _mlp_ple(h.reshape(batch * seq, d), pp, 0, *mlp_weights[0])
    tables_p = _rope_tables(jnp.arange(seq), head_dim)
    k_p, v_p, q_b, k_b, vt_b = _qkv(h.reshape(batch, seq, d), row(kv_norm), row(b_norm[0]),
                                    wk, wv, wq, tables_p, n_heads, head_dim, True)

    xs = x_sample.reshape(dec_batch, d)
    hs, u_s = _mixer_step(xs, state_conv[0, :, 0], state_conv[0, :, 1], row(a_norm[0]), w_in,
                          a_conv_w[0], row(a_conv_b[0]), w_out)
    conv_s = jnp.stack([state_conv[0, :, 1], u_s], axis=1)
    hs = _mlp_ple(hs, ps, 0, *mlp_weights[0])
    tables_s = _rope_tables(jnp.full((dec_batch,), past_len, jnp.int32), head_dim)
    k_s, v_s, q_s = _qkv(hs.reshape(1, dec_batch, d), row(kv_norm), row(b_norm[0]),
                         wk, wv, wq, tables_s, n_heads, head_dim, False)
    k_s, v_s, q_s = (a.reshape(dec_batch, d) for a in (k_s, v_s, q_s))

    o, o_s = _attention(q_b, k_b, vt_b, q_s, k_s, v_s, cache_k, cache_v, page_table, lam_vecs,
                        b_subln[0].astype(F32), n_heads, head_dim, lam_init)
    y_prompt = _mlp_ple(h, pp, 1, *mlp_weights[1],
                        attn=(o.reshape(batch * seq, d), wo), final_g=row(final_norm))
    y_sample = _mlp_ple(hs, ps, 1, *mlp_weights[1],
                        attn=(o_s, wo), final_g=row(final_norm))

    return (y_prompt.reshape(batch, seq, d),
            y_sample.reshape(dec_batch, dec_seq, d),
            conv_p[None],
            conv_s[None],
            k_p.reshape(batch, seq, n_heads, 2, head_dim),
            v_p.reshape(batch, seq, n_heads, d // n_heads),
            k_s.reshape(dec_batch, dec_seq, n_heads, 2, head_dim),
            v_s.reshape(dec_batch, dec_seq, n_heads, d // n_heads))
```
